```python
import math
import jax, jax.numpy as jnp
from jax import lax
import numpy as np

D_MODEL = 2048
BATCH = 2
SEQ = 8192
DEPTH = 4

HEAD_DIM = 64
QUERY_BLOCK = 128
NSA_HEADS = 12
NSA_KV_HEADS = 2
NSA_CMP_LEN = 32
NSA_CMP_STRIDE = 16
NSA_CMP_HIDDEN = 256
NSA_SEL_LEN = 64
NSA_N_SEL = 16
NSA_WINDOW = 512
FORCE_BONUS = 1e4
MASKED_SCORE = -1e9
NEG_INF = -1e30
SB_HEADS = 8
SSM_HEADS = 12
SSM_HEAD_DIM = 64
SSM_GROUPS = 2
SSM_STATE = 128
SSM_CONV = 4
SSM_CHUNK = 128
DT_MIN = 0.001
DT_MAX = 0.1
D_NSA = NSA_HEADS * HEAD_DIM
D_SB = SB_HEADS * HEAD_DIM
D_SSM = SSM_HEADS * SSM_HEAD_DIM
D_MIX = D_NSA + D_SB + D_SSM
CONV_CH = D_SSM + 2 * SSM_GROUPS * SSM_STATE
IN_SIZES = [D_NSA, 6 * NSA_KV_HEADS * HEAD_DIM, NSA_HEADS * 3, 3 * D_SB, D_SSM, CONV_CH, SSM_HEADS]
D_IN = sum(IN_SIZES)
IN_OFFSETS = np.cumsum(IN_SIZES)[:-1].tolist()
N_EXPERTS = 32
N_EXPERT_GROUPS = 4
EXPERTS_PER_GROUP = N_EXPERTS // N_EXPERT_GROUPS
TOP_K = 2
D_FF = 768
EXPERT_BLOCK = 256
PLE_DIM = 256
ALPHA = (2 * DEPTH) ** 0.25
BETA = (8 * DEPTH) ** -0.25

kernel_name = 'hybrid_nsa_stickbreak_ssd_moe_block'


def layer_norm(x, g, b, eps=1e-5):
    xf = x.astype(jnp.float32)
    mu = jnp.mean(xf, -1, keepdims=True)
    var = jnp.mean(jnp.square(xf - mu), -1, keepdims=True)
    return ((xf - mu) * lax.rsqrt(var + eps) * g + b).astype(x.dtype)


def rms_norm(x, g, eps=1e-6):
    xf = x.astype(jnp.float32)
    return (xf * lax.rsqrt(jnp.mean(xf * xf, -1, keepdims=True) + eps) * g).astype(x.dtype)


def alibi_slopes(n):
    return (2.0 ** (-8.0 * np.arange(1, n + 1) / n)).astype(np.float32)


def compress_blocks(kv, pe, w1, w2):
    B_, S, G, dk = kv.shape
    n_cmp = (S - NSA_CMP_LEN) // NSA_CMP_STRIDE + 1
    idx = np.arange(n_cmp)[:, None] * NSA_CMP_STRIDE + np.arange(NSA_CMP_LEN)[None, :]
    blocks = kv[:, idx] + pe[None, None, :, None, :]
    flat = blocks.transpose(0, 3, 1, 2, 4).reshape(B_, G, n_cmp, NSA_CMP_LEN * dk)
    return jax.nn.gelu(flat @ w1) @ w2


def nsa_attention(q, kv, gate_raw, pe_k, w1_k, w2_k, pe_v, w1_v, w2_v):
    B_, S, _ = q.shape
    G, HPG, dk = NSA_KV_HEADS, NSA_HEADS // NSA_KV_HEADS, HEAD_DIM
    f32 = jnp.float32
    nqb = S // QUERY_BLOCK
    n_cmp = (S - NSA_CMP_LEN) // NSA_CMP_STRIDE + 1
    n_slc = S // NSA_SEL_LEN
    n_sel = min(NSA_N_SEL, n_slc)

    qh = (q.reshape(B_, S, G, HPG, dk) * dk ** -0.5).transpose(0, 2, 3, 1, 4)
    q_blocks = qh.reshape(B_, G, HPG, nqb, QUERY_BLOCK, dk).transpose(3, 0, 1, 2, 4, 5)
    kv = kv.reshape(B_, S, 6, G, dk)
    k_cmp_raw, v_cmp_raw, k_slc, v_slc, k_win, v_win = [kv[:, :, j] for j in range(6)]

    k_cmp = compress_blocks(k_cmp_raw, pe_k, w1_k, w2_k)
    v_cmp = compress_blocks(v_cmp_raw, pe_v, w1_v, w2_v)
    cmp_end = jnp.arange(n_cmp) * NSA_CMP_STRIDE + (NSA_CMP_LEN - 1)

    k_sb = k_slc.transpose(0, 2, 1, 3).reshape(B_, G, n_slc, NSA_SEL_LEN, dk)
    v_sb = v_slc.transpose(0, 2, 1, 3).reshape(B_, G, n_slc, NSA_SEL_LEN, dk)
    pad = ((0, 0), (0, 0), (NSA_WINDOW, 0), (0, 0))
    k_wp = jnp.pad(k_win.transpose(0, 2, 1, 3), pad)
    v_wp = jnp.pad(v_win.transpose(0, 2, 1, 3), pad)

    i_c = np.arange(n_cmp)[:, None] * NSA_CMP_STRIDE
    j_s = np.arange(n_slc)[None, :] * NSA_SEL_LEN
    overlap = jnp.asarray(((i_c < j_s + NSA_SEL_LEN) & (i_c + NSA_CMP_LEN > j_s)).astype(np.float32))

    slopes = jnp.asarray(alibi_slopes(NSA_HEADS).reshape(G, HPG))[None, :, :, None, None]
    b_idx = jnp.arange(B_)[:, None, None, None]
    g_idx = jnp.arange(G)[None, :, None, None]
    blk_ids = jnp.arange(n_slc)
    win_off = jnp.arange(NSA_WINDOW + QUERY_BLOCK) - NSA_WINDOW

    def attend(logits, valid, v, spec):
        prob = jax.nn.softmax(jnp.where(valid, logits, NEG_INF), axis=-1) * valid
        return prob, jnp.einsum(spec, prob.astype(v.dtype), v)

    def block(args):
        qb, q0 = args
        t = q0 + jnp.arange(QUERY_BLOCK)
        logit_c = jnp.einsum('bghqd,bgkd->bghqk', qb, k_cmp, preferred_element_type=f32)
        dist_c = (t[:, None] - cmp_end[None, :]).astype(f32)
        p_c, o_c = attend(logit_c - slopes * dist_c, dist_c >= 0, v_cmp, 'bghqk,bgkd->bghqd')
        imp = jnp.einsum('bghqk,kj->bgqj', p_c, overlap)
        cur = (t // NSA_SEL_LEN)[:, None]
        forced = (blk_ids == 0) | (blk_ids == cur) | (blk_ids == cur - 1)
        score = jnp.where(blk_ids <= cur, imp + jnp.where(forced, FORCE_BONUS, 0.0), MASKED_SCORE)
        sel = lax.top_k(score, n_sel)[1]
        k_sel = k_sb[b_idx, g_idx, sel].reshape(B_, G, QUERY_BLOCK, n_sel * NSA_SEL_LEN, dk)
        v_sel = v_sb[b_idx, g_idx, sel].reshape(B_, G, QUERY_BLOCK, n_sel * NSA_SEL_LEN, dk)
        s_pos = (sel[..., None] * NSA_SEL_LEN + jnp.arange(NSA_SEL_LEN)).reshape(B_, G, QUERY_BLOCK, -1)
        dist_s = (t[:, None] - s_pos).astype(f32)[:, :, None]
        logit_s = jnp.einsum('bghqd,bgqkd->bghqk', qb, k_sel, preferred_element_type=f32)
        _, o_s = attend(logit_s - slopes * dist_s, dist_s >= 0, v_sel, 'bghqk,bgqkd->bghqd')
        k_w = lax.dynamic_slice_in_dim(k_wp, q0, NSA_WINDOW + QUERY_BLOCK, axis=2)
        v_w = lax.dynamic_slice_in_dim(v_wp, q0, NSA_WINDOW + QUERY_BLOCK, axis=2)
        w_pos = q0 + win_off
        dist_w = (t[:, None] - w_pos[None, :]).astype(f32)
        valid_w = (dist_w >= 0) & (dist_w < NSA_WINDOW) & (w_pos[None, :] >= 0)
        logit_w = jnp.einsum('bghqd,bgkd->bghqk', qb, k_w, preferred_element_type=f32)
        _, o_w = attend(logit_w - slopes * dist_w, valid_w, v_w, 'bghqk,bgkd->bghqd')
        return o_c, o_s, o_w

    outs = lax.map(block, (q_blocks, jnp.arange(nqb, dtype=jnp.int32) * QUERY_BLOCK))
    o_c, o_s, o_w = [o.transpose(1, 0, 4, 2, 3, 5).reshape(B_, S, NSA_HEADS, dk) for o in outs]
    g = jax.nn.sigmoid(gate_raw.reshape(B_, S, NSA_HEADS, 3))
    o = g[..., 0:1] * o_c + g[..., 1:2] * o_s + g[..., 2:3] * o_w
    return o.reshape(B_, S, D_NSA)


def stick_breaking_attention(qkv):
    B_, S, _ = qkv.shape
    nqb = S // QUERY_BLOCK
    qkv = qkv.reshape(B_, S, 3, SB_HEADS, HEAD_DIM)
    qh = (qkv[:, :, 0] * HEAD_DIM ** -0.5).transpose(0, 2, 1, 3)
    kh = qkv[:, :, 1].transpose(0, 2, 1, 3)
    vh = qkv[:, :, 2].transpose(0, 2, 1, 3)
    q_blocks = qh.reshape(B_, SB_HEADS, nqb, QUERY_BLOCK, HEAD_DIM).transpose(2, 0, 1, 3, 4)
    s_pos = jnp.arange(S)

    def block(args):
        qb, q0 = args
        t = q0 + jnp.arange(QUERY_BLOCK)
        z = jnp.einsum('bhqd,bhkd->bhqk', qb, kh, preferred_element_type=jnp.float32)
        mask = s_pos[None, :] < t[:, None]
        log_1m = jnp.where(mask, -jax.nn.softplus(z), 0.0)
        tail = lax.cumsum(log_1m, axis=3, reverse=True) - log_1m
        a = jnp.where(mask, jnp.exp(jax.nn.log_sigmoid(z) + tail), 0.0)
        return jnp.einsum('bhqk,bhkd->bhqd', a.astype(vh.dtype), vh)

    o = lax.map(block, (q_blocks, jnp.arange(nqb, dtype=jnp.int32) * QUERY_BLOCK))
    return o.transpose(1, 0, 3, 2, 4).reshape(B_, S, D_SB)


def ssd_chunked(x, dt, a, b_in, c_in):
    B_, S, H, P = x.shape
    G, N = b_in.shape[2], b_in.shape[3]
    L, hpg = SSM_CHUNK, H // G
    nc = S // L
    xc = x.reshape(B_, nc, L, G, hpg, P)
    dtc = dt.reshape(B_, nc, L, H)
    bc = b_in.reshape(B_, nc, L, G, N)
    cc = c_in.reshape(B_, nc, L, G, N)
    a_cum = jnp.cumsum(dtc * a, axis=2)
    xdt = xc * dtc.reshape(B_, nc, L, G, hpg)[..., None]
    seg = a_cum[:, :, :, None, :] - a_cum[:, :, None, :, :]
    causal = np.tril(np.ones((L, L), dtype=bool))[None, None, :, :, None]
    decay = jnp.exp(jnp.where(causal, seg, -jnp.inf)).reshape(B_, nc, L, L, G, hpg)
    cb = jnp.einsum('bctgn,bcsgn->bctsg', cc, bc, preferred_element_type=jnp.float32)
    y_diag = jnp.einsum('bctsgh,bcsghp->bctghp', cb[..., None] * decay, xdt)
    decay_states = jnp.exp(a_cum[:, :, -1:, :] - a_cum).reshape(B_, nc, L, G, hpg)
    states = jnp.einsum('bcsgn,bcsghp->bcghpn', bc, xdt * decay_states[..., None])
    chunk_decay = jnp.exp(a_cum[:, :, -1, :]).reshape(B_, nc, G, hpg)

    def step(h, inp):
        st, dec = inp
        return h * dec[..., None, None] + st, h

    h0 = jnp.zeros((B_, G, hpg, P, N), jnp.float32)
    _, h_prev = lax.scan(step, h0, (states.transpose(1, 0, 2, 3, 4, 5), chunk_decay.transpose(1, 0, 2, 3)))
    h_prev = h_prev.transpose(1, 0, 2, 3, 4, 5)
    y_off = jnp.einsum('bctgn,bcghpn->bctghp', cc, h_prev) * jnp.exp(a_cum).reshape(B_, nc, L, G, hpg)[..., None]
    return (y_diag + y_off).reshape(B_, S, H, P).astype(x.dtype)


def mamba2_mixer(z, xbc, dt_raw, conv_w, conv_b, dt_bias, a_log, d_skip, norm_g):
    B_, S, _ = z.shape
    xbc = lax.conv_general_dilated(xbc, conv_w, window_strides=(1,), padding=[(SSM_CONV - 1, 0)],
                                   dimension_numbers=('NWC', 'WIO', 'NWC'), feature_group_count=CONV_CH)
    xbc = jax.nn.silu(xbc + conv_b)
    xs, b_in, c_in = jnp.split(xbc, [D_SSM, D_SSM + SSM_GROUPS * SSM_STATE], axis=-1)
    xs = xs.reshape(B_, S, SSM_HEADS, SSM_HEAD_DIM)
    b_in = b_in.reshape(B_, S, SSM_GROUPS, SSM_STATE)
    c_in = c_in.reshape(B_, S, SSM_GROUPS, SSM_STATE)
    dt = jax.nn.softplus((dt_raw + dt_bias).astype(jnp.float32))
    a = -jnp.exp(a_log.astype(jnp.float32))
    y = ssd_chunked(xs, dt, a, b_in, c_in) + d_skip[:, None] * xs
    return rms_norm(y.reshape(B_, S, D_SSM) * jax.nn.silu(z), norm_g)


def hybrid_mixer(x, w_in, w_out, pe_k, w1_k, w2_k, pe_v, w1_v, w2_v, nsa_g, sb_g,
                 conv_w, conv_b, dt_bias, a_log, d_skip, ssm_g):
    proj = jnp.einsum('bsd,de->bse', x, w_in)
    nsa_q, nsa_kv, nsa_gate, sb_qkv, ssm_z, ssm_xbc, ssm_dt = jnp.split(proj, IN_OFFSETS, axis=-1)
    o_nsa = rms_norm(nsa_attention(nsa_q, nsa_kv, nsa_gate, pe_k, w1_k, w2_k, pe_v, w1_v, w2_v), nsa_g)
    o_sb = rms_norm(stick_breaking_attention(sb_qkv), sb_g)
    o_ssm = mamba2_mixer(ssm_z, ssm_xbc, ssm_dt, conv_w, conv_b, dt_bias, a_log, d_skip, ssm_g)
    o = jnp.concatenate([o_nsa, o_sb, o_ssm], axis=-1)
    return jnp.einsum('bse,ed->bsd', o, w_out)


def group_limited_moe(h, router_w, router_b, w_gate, w_up, w_down):
    B_, S, D = h.shape
    T = B_ * S
    xf = h.reshape(T, D)
    logits = jnp.einsum('td,de->te', xf, router_w, preferred_element_type=jnp.float32) + router_b
    probs = jax.nn.softmax(logits, axis=-1).reshape(T, N_EXPERT_GROUPS, EXPERTS_PER_GROUP)
    group_score = jnp.sum(lax.top_k(probs, TOP_K)[0], axis=-1)
    g_sel = jnp.argmax(group_score, axis=-1)
    in_group = jnp.take_along_axis(probs, g_sel[:, None, None], axis=1)[:, 0]
    w_top, e_local = lax.top_k(in_group, TOP_K)
    w_top = w_top / jnp.sum(w_top, axis=-1, keepdims=True)
    e_top = g_sel[:, None] * EXPERTS_PER_GROUP + e_local
    TK = T * TOP_K
    flat_e = e_top.reshape(TK)
    flat_w = w_top.reshape(TK)
    flat_tok = jnp.repeat(jnp.arange(T, dtype=jnp.int32), TOP_K)
    order = jnp.argsort(flat_e)
    se = flat_e[order]
    counts = jnp.zeros((N_EXPERTS,), jnp.int32).at[flat_e].add(1)
    padded = (counts + EXPERT_BLOCK - 1) // EXPERT_BLOCK * EXPERT_BLOCK
    start = jnp.cumsum(counts) - counts
    ends = jnp.cumsum(padded)
    pstart = ends - padded
    dest = pstart[se] + jnp.arange(TK, dtype=jnp.int32) - start[se]
    n_blocks = -(-TK // EXPERT_BLOCK) + N_EXPERTS
    n_rows = n_blocks * EXPERT_BLOCK
    row_tok = jnp.full((n_rows,), T, jnp.int32).at[dest].set(flat_tok[order])
    row_w = jnp.zeros((n_rows,), jnp.float32).at[dest].set(flat_w[order])
    block_e = jnp.minimum(jnp.searchsorted(ends, jnp.arange(n_blocks) * EXPERT_BLOCK, side='right'), N_EXPERTS - 1)
    x_pad = jnp.concatenate([xf, jnp.zeros((1, D), xf.dtype)], axis=0)
    xb = x_pad[row_tok].reshape(n_blocks, EXPERT_BLOCK, D)

    def expert_block(args):
        xblk, e = args
        hid = jax.nn.silu(xblk @ w_gate[e]) * (xblk @ w_up[e])
        return hid @ w_down[e]

    yb = lax.map(expert_block, (xb, block_e)).reshape(n_rows, D)
    out = jnp.zeros((T + 1, D), jnp.float32).at[row_tok].add(yb.astype(jnp.float32) * row_w[:, None])
    return out[:T].reshape(B_, S, D).astype(h.dtype)


def setup_inputs(seed: int = 0) -> dict:
    key = jax.random.key(seed)
    ks = jax.random.split(key, 32)
    f32 = jnp.float32

    def nrm(k, shape, scale):
        return jax.random.normal(k, shape, f32) * scale

    def gain(k, shape):
        return 1.0 + 0.02 * jax.random.normal(k, shape, f32)

    L = DEPTH
    flat_blk = NSA_CMP_LEN * HEAD_DIM
    dt = jnp.exp(jax.random.uniform(ks[14], (L, SSM_HEADS), f32) * (math.log(DT_MAX) - math.log(DT_MIN)) + math.log(DT_MIN))
    return {
        'x': nrm(ks[0], (BATCH, SEQ, D_MODEL), 1.0),
        'p': nrm(ks[1], (DEPTH, BATCH, SEQ, PLE_DIM), 1.0),
        'w_in': nrm(ks[2], (L, D_MODEL, D_IN), D_MODEL ** -0.5),
        'w_out': nrm(ks[3], (L, D_MIX, D_MODEL), BETA * D_MIX ** -0.5),
        'cmp_pe_k': nrm(ks[4], (L, NSA_CMP_LEN, HEAD_DIM), 0.02),
        'cmp_w1_k': nrm(ks[5], (L, flat_blk, NSA_CMP_HIDDEN), flat_blk ** -0.5),
        'cmp_w2_k': nrm(ks[6], (L, NSA_CMP_HIDDEN, HEAD_DIM), NSA_CMP_HIDDEN ** -0.5),
        'cmp_pe_v': nrm(ks[7], (L, NSA_CMP_LEN, HEAD_DIM), 0.02),
        'cmp_w1_v': nrm(ks[8], (L, flat_blk, NSA_CMP_HIDDEN), flat_blk ** -0.5),
        'cmp_w2_v': nrm(ks[9], (L, NSA_CMP_HIDDEN, HEAD_DIM), NSA_CMP_HIDDEN ** -0.5),
        'nsa_norm_g': gain(ks[10], (L, D_NSA)),
        'sb_norm_g': gain(ks[11], (L, D_SB)),
        'conv_w': nrm(ks[12], (L, SSM_CONV, 1, CONV_CH), SSM_CONV ** -0.5),
        'conv_b': nrm(ks[13], (L, CONV_CH), 0.02),
        'dt_bias': dt + jnp.log(-jnp.expm1(-dt)),
        'a_log': jnp.log(jax.random.uniform(ks[15], (L, SSM_HEADS), f32, 1.0, 16.0)),
        'd_skip': gain(ks[16], (L, SSM_HEADS)),
        'ssm_norm_g': gain(ks[17], (L, D_SSM)),
        'ln1_g': gain(ks[18], (L, D_MODEL)),
        'ln1_b': nrm(ks[19], (L, D_MODEL), 0.02),
        'ln2_g': gain(ks[20], (L, D_MODEL)),
        'ln2_b': nrm(ks[21], (L, D_MODEL), 0.02),
        'router_w': nrm(ks[22], (D_MODEL, N_EXPERTS), D_MODEL ** -0.5),
        'router_b': nrm(ks[23], (N_EXPERTS,), 0.01),
        'expert_w_gate': nrm(ks[24], (L, N_EXPERTS, D_MODEL, D_FF), D_MODEL ** -0.5),
        'expert_w_up': nrm(ks[25], (L, N_EXPERTS, D_MODEL, D_FF), D_MODEL ** -0.5),
        'expert_w_down': nrm(ks[26], (L, N_EXPERTS, D_FF, D_MODEL), BETA * D_FF ** -0.5),
        'ple_gate_w': nrm(ks[27], (L, D_MODEL, D_MODEL), D_MODEL ** -0.5),
        'ple_proj_w': nrm(ks[28], (L, PLE_DIM, D_MODEL), BETA * PLE_DIM ** -0.5),
    }


def reference(x, p, w_in, w_out, cmp_pe_k, cmp_w1_k, cmp_w2_k, cmp_pe_v, cmp_w1_v, cmp_w2_v,
              nsa_norm_g, sb_norm_g, conv_w, conv_b, dt_bias, a_log, d_skip, ssm_norm_g,
              ln1_g, ln1_b, ln2_g, ln2_b, router_w, router_b, expert_w_gate, expert_w_up,
              expert_w_down, ple_gate_w, ple_proj_w):
    for i in range(DEPTH):
        mix = hybrid_mixer(x, w_in[i], w_out[i], cmp_pe_k[i], cmp_w1_k[i], cmp_w2_k[i],
                           cmp_pe_v[i], cmp_w1_v[i], cmp_w2_v[i], nsa_norm_g[i], sb_norm_g[i],
                           conv_w[i], conv_b[i], dt_bias[i], a_log[i], d_skip[i], ssm_norm_g[i])
        h = layer_norm(ALPHA * x + mix, ln1_g[i], ln1_b[i])
        ffn = group_limited_moe(h, router_w, router_b, expert_w_gate[i], expert_w_up[i], expert_w_down[i])
        ple = jax.nn.sigmoid(jnp.einsum('bsd,de->bse', h, ple_gate_w[i])) * jnp.einsum('bsk,kd->bsd', p[i], ple_proj_w[i])
        x = layer_norm(ALPHA * h + ffn + ple, ln2_g[i], ln2_b[i])
    return x
```

```python
import functools
import math

import jax
import jax.numpy as jnp
import numpy as np
from jax import lax
from jax.experimental import pallas as pl
from jax.experimental.pallas import tpu as pltpu

F32 = jnp.float32
BF16 = jnp.bfloat16

HEAD_DIM = 64
NSA_HEADS = 12
NSA_KV_HEADS = 2
NSA_HPG = NSA_HEADS // NSA_KV_HEADS
NSA_CMP_LEN = 32
NSA_CMP_STRIDE = 16
NSA_CMP_HIDDEN = 256
NSA_SEL_LEN = 64
NSA_N_SEL = 16
NSA_WINDOW = 512
FORCE_BONUS = 1e4
MASKED_SCORE = -1e9
NEG_INF = -1e30
SB_HEADS = 8
SSM_HEADS = 12
SSM_HEAD_DIM = 64
SSM_GROUPS = 2
SSM_STATE = 128
SSM_CONV = 4
D_NSA = NSA_HEADS * HEAD_DIM
D_SB = SB_HEADS * HEAD_DIM
D_SSM = SSM_HEADS * SSM_HEAD_DIM
CONV_CH = D_SSM + 2 * SSM_GROUPS * SSM_STATE
N_EXPERTS = 32
N_EXPERT_GROUPS = 4
EXPERTS_PER_GROUP = N_EXPERTS // N_EXPERT_GROUPS
TOP_K = 2

LANES = 128
SUBLANES = 8
VMEM_LIMIT_BYTES = 56 * 1024 * 1024

COL_Q = 0
COL_KV = COL_Q + D_NSA
COL_SBQ = COL_KV + 6 * LANES
COL_SBK = COL_SBQ + D_SB
COL_SBV = COL_SBK + D_SB
COL_Z = COL_SBV + D_SB
COL_XBC = COL_Z + D_SSM
N_MAIN = COL_XBC + CONV_CH
MISC_GATE = 0
MISC_DT = NSA_HEADS * 3

QB = 128
SEL_CHUNK = 512
SB_TQ = 256
SSD_L = 128
MM_TM = 256
EXPERT_BLOCK = 256
SB_SKIP_BELOW = -150.0


def _cparams(sem):
    return pltpu.CompilerParams(dimension_semantics=sem, vmem_limit_bytes=VMEM_LIMIT_BYTES)


def _split3(x):
    hi = x.astype(BF16)
    r1 = x - hi.astype(F32)
    mid = r1.astype(BF16)
    lo = (r1 - mid.astype(F32)).astype(BF16)
    return hi, mid, lo


def _dot(a, b):
    return jnp.dot(a, b, preferred_element_type=F32)


def _dot_nt(a, b):
    return lax.dot_general(a, b, (((1,), (1,)), ((), ())), preferred_element_type=F32)


def _inproj_kernel(x_ref, w_ref, wm_ref, o_ref, m_ref, xb_ref):
    @pl.when(pl.program_id(1) == 0)
    def _():
        xb_ref[...] = x_ref[...].astype(BF16)
        m_ref[...] = _dot(xb_ref[...], wm_ref[...])

    o_ref[...] = _dot(xb_ref[...], w_ref[...]).astype(o_ref.dtype)


def _inproj(x2d, w_main, w_misc, tm=1024, tn=640):
    T, D = x2d.shape
    tm = min(tm, T)
    return pl.pallas_call(
        _inproj_kernel,
        grid=(T // tm, N_MAIN // tn),
        in_specs=[pl.BlockSpec((tm, D), lambda i, j: (i, 0)),
                  pl.BlockSpec((D, tn), lambda i, j: (0, j)),
                  pl.BlockSpec((D, LANES), lambda i, j: (0, 0))],
        out_specs=[pl.BlockSpec((tm, tn), lambda i, j: (i, j)),
                   pl.BlockSpec((tm, LANES), lambda i, j: (i, 0))],
        out_shape=[jax.ShapeDtypeStruct((T, N_MAIN), BF16),
                   jax.ShapeDtypeStruct((T, LANES), F32)],
        scratch_shapes=[pltpu.VMEM((tm, D), BF16)],
        compiler_params=_cparams(("parallel", "arbitrary")),
        name="inproj",
    )(x2d, w_main, w_misc)


def _gelu_tanh(x):
    return 0.5 * x * (1.0 + jnp.tanh(math.sqrt(2.0 / math.pi) * (x + 0.044715 * (x * x * x))))


def _cmp_kernel(x_ref, pe_ref, w1_ref, w2_ref, o_ref):
    nc = x_ref.shape[0]
    half = x_ref.shape[1]
    w1 = w1_ref[...].astype(BF16)
    pq = _dot(x_ref[...], w1[:half]), _dot(x_ref[...], w1[half:])
    pe_term = _dot(pe_ref[...].astype(BF16), w1)[0:1]
    h = pq[0] + pltpu.roll(pq[1], nc - 1, axis=0) + pe_term
    o_ref[...] = _dot(_gelu_tanh(h).astype(BF16), w2_ref[...].astype(BF16)).astype(o_ref.dtype)


def _nsa_compress(xr, pe, w1, w2):
    _, B, G, NC, W = xr.shape
    return pl.pallas_call(
        _cmp_kernel,
        grid=(2, B, G),
        in_specs=[pl.BlockSpec((None, None, None, NC, W), lambda s, b, g: (s, b, g, 0, 0)),
                  pl.BlockSpec((None, SUBLANES, 2 * W), lambda s, b, g: (s, 0, 0)),
                  pl.BlockSpec((None, 2 * W, NSA_CMP_HIDDEN), lambda s, b, g: (s, 0, 0)),
                  pl.BlockSpec((None, NSA_CMP_HIDDEN, HEAD_DIM), lambda s, b, g: (s, 0, 0))],
        out_specs=pl.BlockSpec((None, None, None, NC, HEAD_DIM), lambda s, b, g: (s, b, g, 0, 0)),
        out_shape=jax.ShapeDtypeStruct((2, B, G, NC, HEAD_DIM), BF16),
        compiler_params=_cparams(("parallel", "parallel", "parallel")),
        name="nsa_compress",
    )(xr, pe, w1, w2)


def _alibi_slopes():
    return [float(2.0 ** (-8.0 * (i + 1) / NSA_HEADS)) for i in range(NSA_HEADS)]


def _softmax_rows(lg, valid):
    m = jnp.max(lg, axis=-1, keepdims=True)
    e = jnp.where(valid, jnp.exp(lg - m), 0.0)
    s = jnp.sum(e, axis=-1, keepdims=True)
    return e / jnp.maximum(s, 1e-30)


def _nsa_kernel(q_ref, misc_ref, kc_ref, vc_ref, ks_ref, vs_ref, kw_ref, vw_ref, o_ref,
                m_sc, l_sc, acc_sc, *, seq):
    H = NSA_HPG
    R = H * QB
    nc = kc_ref.shape[0]
    nb = seq // NSA_SEL_LEN
    n_sel = min(NSA_N_SEL, nb)
    bpc = SEL_CHUNK // NSA_SEL_LEN
    q0 = pl.program_id(1) * QB
    t_q = q0 + lax.broadcasted_iota(jnp.int32, (QB, 1), 0)
    lane = lax.broadcasted_iota(jnp.int32, (QB, LANES), 1)
    gates = jax.nn.sigmoid(misc_ref[...])
    slopes = _alibi_slopes()

    for g in range(NSA_KV_HEADS):
        in_g = (lane >= g * HEAD_DIM) & (lane < (g + 1) * HEAD_DIM)
        parts = []
        for h in range(H):
            hh = g * H + h
            slab = q_ref[:, (hh // 2) * LANES:(hh // 2 + 1) * LANES].astype(F32)
            if hh % 2 != g:
                slab = pltpu.roll(slab, HEAD_DIM, axis=1)
            parts.append(jnp.where(in_g, slab, 0.0).astype(BF16))
        qp = jnp.concatenate(parts, axis=0)
        slope3 = jnp.concatenate(
            [jnp.full((1, 1, 1), slopes[g * H + h], F32) for h in range(H)], axis=0)
        t3 = t_q[None]

        cmp_end = lax.broadcasted_iota(jnp.int32, (1, 1, nc), 2) * NSA_CMP_STRIDE + (NSA_CMP_LEN - 1)
        dist_c = (t3 - cmp_end).astype(F32)
        valid_c = dist_c >= 0
        lc = _dot_nt(qp, kc_ref[...]).reshape(H, QB, nc)
        p_c = _softmax_rows(jnp.where(valid_c, lc - slope3 * dist_c, NEG_INF), valid_c)
        o_c = _dot(p_c.reshape(R, nc).astype(BF16), vc_ref[...])

        p_sum = jnp.sum(p_c, axis=0)
        kk = lax.broadcasted_iota(jnp.int32, (nc, nb), 0)
        jj = lax.broadcasted_iota(jnp.int32, (nc, nb), 1)
        ratio = NSA_SEL_LEN // NSA_CMP_STRIDE
        overlap = ((kk < ratio * jj + ratio) & (kk + NSA_CMP_LEN // NSA_CMP_STRIDE > ratio * jj)
                   & (kk < nc - 1)).astype(BF16)
        imp = sum(_dot(piece, overlap) for piece in _split3(p_sum))
        blk = lax.broadcasted_iota(jnp.int32, (QB, nb), 1)
        cur = t_q // NSA_SEL_LEN
        forced = (blk == 0) | (blk == cur) | (blk == cur - 1)
        score = jnp.where(blk <= cur, imp + jnp.where(forced, FORCE_BONUS, 0.0), MASKED_SCORE)
        sel = jnp.zeros((QB, nb), F32)
        for _ in range(n_sel):
            mx = jnp.max(score, axis=-1, keepdims=True)
            first = jnp.min(jnp.where(score == mx, blk, nb), axis=-1, keepdims=True)
            pick = blk == first
            sel = jnp.where(pick, 1.0, sel)
            score = jnp.where(pick, -3e38, score)
        sel_bf = sel.astype(BF16)

        m_sc[...] = jnp.full(m_sc.shape, NEG_INF, F32)
        l_sc[...] = jnp.zeros(l_sc.shape, F32)
        acc_sc[...] = jnp.zeros(acc_sc.shape, F32)
        n_chunks = (q0 + QB + SEL_CHUNK - 1) // SEL_CHUNK

        def sel_step(c, carry):
            in_chunk = (blk // bpc) == c
            hit = jnp.max(jnp.where(in_chunk, sel, 0.0))

            @pl.when(hit > 0.0)
            def _():
                k0 = pl.multiple_of(c * SEL_CHUNK, SEL_CHUNK)
                ej = lax.broadcasted_iota(jnp.int32, (nb, SEL_CHUNK), 0)
                es = lax.broadcasted_iota(jnp.int32, (nb, SEL_CHUNK), 1)
                expand = (ej == c * bpc + es // NSA_SEL_LEN).astype(BF16)
                key_sel = _dot(sel_bf, expand)
                s_pos = k0 + lax.broadcasted_iota(jnp.int32, (1, 1, SEL_CHUNK), 2)
                dist = (t3 - s_pos).astype(F32)
                valid = (key_sel[None] > 0.5) & (dist >= 0)
                ls = _dot_nt(qp, ks_ref[pl.ds(k0, SEL_CHUNK), :]).reshape(H, QB, SEL_CHUNK)
                lg = jnp.where(valid, ls - slope3 * dist, NEG_INF)
                m_old = m_sc[...].reshape(H, QB, 1)
                m_new = jnp.maximum(m_old, jnp.max(lg, axis=-1, keepdims=True))
                alpha = jnp.exp(m_old - m_new)
                e = jnp.where(valid, jnp.exp(lg - m_new), 0.0)
                l_new = alpha * l_sc[...].reshape(H, QB, 1) + jnp.sum(e, axis=-1, keepdims=True)
                pv = _dot(e.reshape(R, SEL_CHUNK).astype(BF16), vs_ref[pl.ds(k0, SEL_CHUNK), :])
                acc_sc[...] = alpha.reshape(R, 1) * acc_sc[...] + pv
                m_sc[...] = m_new.reshape(R, 1)
                l_sc[...] = l_new.reshape(R, 1)

            return carry

        lax.fori_loop(0, n_chunks, sel_step, 0)
        o_s = acc_sc[...] / jnp.maximum(l_sc[...], 1e-30)

        wk = NSA_WINDOW + QB
        w0 = pl.multiple_of(jnp.maximum(q0 - NSA_WINDOW, 0), QB)
        w_pos = w0 + lax.broadcasted_iota(jnp.int32, (1, 1, wk), 2)
        dist_w = (t3 - w_pos).astype(F32)
        valid_w = (dist_w >= 0) & (dist_w < NSA_WINDOW)
        lw = _dot_nt(qp, kw_ref[pl.ds(w0, wk), :]).reshape(H, QB, wk)
        p_w = _softmax_rows(jnp.where(valid_w, lw - slope3 * dist_w, NEG_INF), valid_w)
        o_w = _dot(p_w.reshape(R, wk).astype(BF16), vw_ref[pl.ds(w0, wk), :])

        outs = []
        for h in range(H):
            hh = g * H + h
            rows = slice(h * QB, (h + 1) * QB)
            gc = gates[:, hh * 3 + 0:hh * 3 + 1]
            gs = gates[:, hh * 3 + 1:hh * 3 + 2]
            gw = gates[:, hh * 3 + 2:hh * 3 + 3]
            outs.append(gc * o_c[rows] + gs * o_s[rows] + gw * o_w[rows])
        for pair in range(H // 2):
            a, b = outs[2 * pair], outs[2 * pair + 1]
            if g == 0:
                slab = jnp.where(lane < HEAD_DIM, a, pltpu.roll(b, HEAD_DIM, axis=1))
            else:
                slab = jnp.where(lane < HEAD_DIM, pltpu.roll(a, HEAD_DIM, axis=1), b)
            col = (g * H // 2 + pair) * LANES
            o_ref[:, col:col + LANES] = slab.astype(o_ref.dtype)


def _nsa_attention(proj, misc, kcmp, vcmp, B, S):
    T = B * S
    nqb = S // QB
    NC = kcmp.shape[1]
    R = NSA_HPG * QB
    qblk = D_NSA // LANES
    kv0 = COL_KV // LANES

    def slab(j):
        return pl.BlockSpec((S, LANES), lambda b, i, j=j: (b, kv0 + j))

    return pl.pallas_call(
        functools.partial(_nsa_kernel, seq=S),
        grid=(B, nqb),
        in_specs=[pl.BlockSpec((QB, D_NSA), lambda b, i: (b * nqb + i, 0)),
                  pl.BlockSpec((QB, LANES), lambda b, i: (b * nqb + i, 0)),
                  pl.BlockSpec((None, NC, LANES), lambda b, i: (b, 0, 0)),
                  pl.BlockSpec((None, NC, LANES), lambda b, i: (b, 0, 0)),
                  slab(2), slab(3), slab(4), slab(5)],
        out_specs=pl.BlockSpec((QB, D_NSA), lambda b, i: (b * nqb + i, 0)),
        out_shape=jax.ShapeDtypeStruct((T, D_NSA), F32),
        scratch_shapes=[pltpu.VMEM((R, 1), F32), pltpu.VMEM((R, 1), F32), pltpu.VMEM((R, LANES), F32)],
        compiler_params=_cparams(("parallel", "arbitrary")),
        name="nsa_attention",
    )(proj, misc, kcmp, vcmp, proj, proj, proj, proj)


def _softplus(z):
    return jnp.maximum(z, 0.0) + jnp.log(1.0 + jnp.exp(-jnp.abs(z)))


def _sb_kernel(q_ref, k_ref, v_ref, o_ref, c_sc, acc_sc):
    tq = q_ref.shape[0]
    qi = pl.program_id(2)
    lane = lax.broadcasted_iota(jnp.int32, (tq, LANES), 1)
    jr = lax.broadcasted_iota(jnp.int32, (tq, tq), 0)
    sc = lax.broadcasted_iota(jnp.int32, (tq, tq), 1)
    upper = (jr > sc).astype(BF16)
    diag_mask = sc < jr
    qf = q_ref[...].astype(F32)
    res = []
    for h in range(2):
        in_h = (lane >= h * HEAD_DIM) & (lane < (h + 1) * HEAD_DIM)
        qp = jnp.where(in_h, qf, 0.0).astype(BF16)
        c_sc[...] = jnp.zeros(c_sc.shape, F32)
        acc_sc[...] = jnp.zeros(acc_sc.shape, F32)

        def block(kb, mask):
            k0 = pl.multiple_of(kb * tq, tq)
            z = _dot_nt(qp, k_ref[pl.ds(k0, tq), :])
            sp = _softplus(z)
            log1m = -sp if mask is None else jnp.where(mask, -sp, 0.0)
            hi, mid, lo = _split3(log1m)
            tail = _dot(hi, upper) + _dot(mid, upper) + _dot(lo, upper)
            arg = (z - sp) + tail + c_sc[...]
            a = jnp.exp(arg)
            if mask is not None:
                a = jnp.where(mask, a, 0.0)
            acc_sc[...] += _dot(a.astype(BF16), v_ref[pl.ds(k0, tq), :])
            c_sc[...] += jnp.sum(log1m, axis=-1, keepdims=True)

        block(qi, diag_mask)

        def cond(st):
            kb, cmax = st
            return (kb >= 0) & (cmax > SB_SKIP_BELOW)

        def body(st):
            kb, _ = st
            block(kb, None)
            return kb - 1, jnp.max(c_sc[...])

        lax.while_loop(cond, body, (qi - 1, jnp.max(c_sc[...])))
        res.append(acc_sc[...])
    o_ref[...] = jnp.where(lane < HEAD_DIM, res[0], res[1]).astype(o_ref.dtype)


def _sb_attention(proj, B, S):
    T = B * S
    tq = min(SB_TQ, S)
    nq = S // tq
    nslab = D_SB // LANES
    return pl.pallas_call(
        _sb_kernel,
        grid=(B, nslab, nq),
        in_specs=[pl.BlockSpec((tq, LANES), lambda b, s, i: (b * nq + i, COL_SBQ // LANES + s)),
                  pl.BlockSpec((S, LANES), lambda b, s, i: (b, COL_SBK // LANES + s)),
                  pl.BlockSpec((S, LANES), lambda b, s, i: (b, COL_SBV // LANES + s))],
        out_specs=pl.BlockSpec((tq, LANES), lambda b, s, i: (b * nq + i, s)),
        out_shape=jax.ShapeDtypeStruct((T, D_SB), F32),
        scratch_shapes=[pltpu.VMEM((tq, 1), F32), pltpu.VMEM((tq, LANES), F32)],
        compiler_params=_cparams(("parallel", "parallel", "arbitrary")),
        name="sb_attention",
    )(proj, proj, proj)


def _ssd_kernel(z_ref, xbc_ref, misc_ref, cw_ref, cb_ref, dtb_ref, alog_ref, dskip_ref, g_ref,
                o_ref, xbuf, state):
    L = SSD_L
    c = pl.program_id(1)

    @pl.when(c == 0)
    def _():
        xbuf[0:SUBLANES, :] = jnp.zeros((SUBLANES, CONV_CH), F32)
        state[...] = jnp.zeros(state.shape, F32)

    xbuf[SUBLANES:SUBLANES + L, :] = xbc_ref[...].astype(F32)
    conv = cb_ref[...]
    for w in range(SSM_CONV):
        off = SUBLANES - (SSM_CONV - 1) + w
        conv = conv + xbuf[off:off + L, :] * cw_ref[w:w + 1, :]
    xbuf[0:SUBLANES, :] = xbuf[L:L + SUBLANES, :]
    xbc = conv * jax.nn.sigmoid(conv)
    xs = xbc[:, :D_SSM]
    b_in = [xbc[:, D_SSM + g * SSM_STATE:D_SSM + (g + 1) * SSM_STATE] for g in range(SSM_GROUPS)]
    c0 = D_SSM + SSM_GROUPS * SSM_STATE
    c_in = [xbc[:, c0 + g * SSM_STATE:c0 + (g + 1) * SSM_STATE].astype(BF16) for g in range(SSM_GROUPS)]

    dt = _softplus(misc_ref[...] + dtb_ref[...])
    a_neg = -jnp.exp(alog_ref[...])
    d_a = dt * a_neg
    rr = lax.broadcasted_iota(jnp.int32, (L, L), 0)
    cc = lax.broadcasted_iota(jnp.int32, (L, L), 1)
    causal = cc <= rr
    tril = causal.astype(BF16)
    a_cum = sum(_dot(tril, piece) for piece in _split3(d_a))
    a_cum_t = a_cum.T
    lane = lax.broadcasted_iota(jnp.int32, (L, LANES), 1)
    b_t = [b.T.astype(F32) for b in b_in]
    cb = [_dot_nt(c_in[g], b_in[g].astype(BF16)) for g in range(SSM_GROUPS)]

    ys = []
    for pair in range(SSM_HEADS // 2):
        h0, h1 = 2 * pair, 2 * pair + 1
        dt_pair = jnp.where(lane < SSM_HEAD_DIM, dt[:, MISC_DT + h0:MISC_DT + h0 + 1],
                            dt[:, MISC_DT + h1:MISC_DT + h1 + 1])
        xs_pair = xs[:, pair * LANES:(pair + 1) * LANES]
        xdt = (xs_pair * dt_pair).astype(BF16)
        y_pair = []
        for h in (h0, h1):
            g = h // (SSM_HEADS // SSM_GROUPS)
            col = a_cum[:, MISC_DT + h:MISC_DT + h + 1]
            row = a_cum_t[MISC_DT + h:MISC_DT + h + 1, :]
            last = row[:, L - 1:L]
            decay = jnp.exp(jnp.where(causal, col - row, NEG_INF))
            y = _dot((cb[g] * decay).astype(BF16), xdt)
            y = y + _dot(c_in[g], state[h].astype(BF16)) * jnp.exp(col)
            new_state = state[h] * jnp.exp(last) + _dot((b_t[g] * jnp.exp(last - row)).astype(BF16), xdt)
            state[h] = new_state
            y_pair.append(y)
        ys.append(jnp.where(lane < SSM_HEAD_DIM, y_pair[0], y_pair[1]))
    y = jnp.concatenate(ys, axis=1) + dskip_ref[...] * xs
    zf = z_ref[...].astype(F32)
    y = y * (zf * jax.nn.sigmoid(zf))
    y = y * lax.rsqrt(jnp.mean(y * y, axis=-1, keepdims=True) + 1e-6) * g_ref[...]
    o_ref[...] = y.astype(o_ref.dtype)


def _ssd_mixer(proj, misc, conv_w, conv_b, dtb_pad, alog_pad, dskip_full, norm_g, B, S):
    T = B * S
    nc = S // SSD_L

    def row(width):
        return pl.BlockSpec((1, width), lambda b, c: (0, 0))

    return pl.pallas_call(
        _ssd_kernel,
        grid=(B, nc),
        in_specs=[pl.BlockSpec((SSD_L, D_SSM), lambda b, c: (b * nc + c, COL_Z // D_SSM)),
                  pl.BlockSpec((SSD_L, CONV_CH), lambda b, c: (b * nc + c, COL_XBC // CONV_CH)),
                  pl.BlockSpec((SSD_L, LANES), lambda b, c: (b * nc + c, 0)),
                  pl.BlockSpec((SUBLANES, CONV_CH), lambda b, c: (0, 0)),
                  row(CONV_CH), row(LANES), row(LANES), row(D_SSM), row(D_SSM)],
        out_specs=pl.BlockSpec((SSD_L, D_SSM), lambda b, c: (b * nc + c, 0)),
        out_shape=jax.ShapeDtypeStruct((T, D_SSM), F32),
        scratch_shapes=[pltpu.VMEM((SSD_L + SUBLANES, CONV_CH), F32),
                        pltpu.VMEM((SSM_HEADS, SSM_STATE, LANES), F32)],
        compiler_params=_cparams(("parallel", "arbitrary")),
        name="ssd_mixer",
    )(proj, proj, misc, conv_w, conv_b, dtb_pad, alog_pad, dskip_full, norm_g)


def _layer_norm_rows(v, g, b):
    mu = jnp.mean(v, axis=-1, keepdims=True)
    d = v - mu
    var = jnp.mean(d * d, axis=-1, keepdims=True)
    return d * lax.rsqrt(var + 1e-5) * g + b


def _rms_rows(v, g):
    return v * lax.rsqrt(jnp.mean(v * v, axis=-1, keepdims=True) + 1e-6) * g


def _route(logits_t):
    tm = logits_t.shape[1]
    m = jnp.max(logits_t, axis=0, keepdims=True)
    e = jnp.exp(logits_t - m)
    probs = e / jnp.sum(e, axis=0, keepdims=True)
    p3 = probs.reshape(N_EXPERT_GROUPS, EXPERTS_PER_GROUP, tm)
    io_e = lax.broadcasted_iota(jnp.int32, p3.shape, 1)

    def top2(p, io, n):
        m1 = jnp.max(p, axis=-2, keepdims=True)
        i1 = jnp.min(jnp.where(p == m1, io, n), axis=-2, keepdims=True)
        rest = jnp.where(io == i1, -1.0, p)
        m2 = jnp.max(rest, axis=-2, keepdims=True)
        i2 = jnp.min(jnp.where(rest == m2, io, n), axis=-2, keepdims=True)
        return m1, i1, m2, i2

    m1, _, m2, _ = top2(p3, io_e, EXPERTS_PER_GROUP)
    gscore = (m1 + m2)[:, 0, :]
    io_g = lax.broadcasted_iota(jnp.int32, gscore.shape, 0)
    gmax = jnp.max(gscore, axis=0, keepdims=True)
    gsel = jnp.min(jnp.where(gscore == gmax, io_g, N_EXPERT_GROUPS), axis=0, keepdims=True)
    in_group = jnp.sum(jnp.where(io_g[:, None, :] == gsel[None], p3, 0.0), axis=0)
    io_l = lax.broadcasted_iota(jnp.int32, in_group.shape, 0)
    w1, e1, w2, e2 = top2(in_group, io_l, EXPERTS_PER_GROUP)
    wsum = w1 + w2
    ids = jnp.concatenate([gsel * EXPERTS_PER_GROUP + e1, gsel * EXPERTS_PER_GROUP + e2], axis=0)
    wts = jnp.concatenate([w1 / wsum, w2 / wsum], axis=0)
    return ids, wts


def _outproj_kernel(onsa_ref, osb_ref, ossm_ref, x_ref, w_ref, gn_ref, gs_ref, lg_ref, lb_ref,
                    rw_ref, rb_ref, h_ref, hb_ref, ids_ref, wts_ref, *, alpha):
    on = _rms_rows(onsa_ref[...], gn_ref[...]).astype(BF16)
    os_ = _rms_rows(osb_ref[...], gs_ref[...]).astype(BF16)
    om = ossm_ref[...].astype(BF16)
    mix = (_dot(on, w_ref[0:D_NSA, :]) + _dot(os_, w_ref[D_NSA:D_NSA + D_SB, :])
           + _dot(om, w_ref[D_NSA + D_SB:, :]))
    h = _layer_norm_rows(alpha * x_ref[...] + mix, lg_ref[...], lb_ref[...])
    h_ref[...] = h
    hb_ref[...] = h.astype(BF16)
    h_hi, h_mid, _ = _split3(h)
    r_hi, r_mid, _ = _split3(rw_ref[...])
    logits_t = (_dot_nt(r_hi, h_hi) + _dot_nt(r_hi, h_mid) + _dot_nt(r_mid, h_hi)) + rb_ref[...]
    ids, wts = _route(logits_t)
    pad_i = jnp.zeros((SUBLANES - TOP_K, ids.shape[1]), jnp.int32)
    ids_ref[...] = jnp.concatenate([ids, pad_i], axis=0)
    wts_ref[...] = jnp.concatenate([wts, pad_i.astype(F32)], axis=0)


def _outproj_ln_router(onsa, osb, ossm, x2d, w_out, gn, gs, lg, lb, rw_t, rb, alpha):
    T, D = x2d.shape
    tm = min(MM_TM, T)

    def rows(width):
        return pl.BlockSpec((tm, width), lambda i: (i, 0))

    def const(shape):
        return pl.BlockSpec(shape, lambda i: (0,) * len(shape))

    return pl.pallas_call(
        functools.partial(_outproj_kernel, alpha=alpha),
        grid=(T // tm,),
        in_specs=[rows(D_NSA), rows(D_SB), rows(D_SSM), rows(D),
                  const((D, D)), const((1, D_NSA)), const((1, D_SB)), const((1, D)), const((1, D)),
                  const((N_EXPERTS, D)), const((N_EXPERTS, 1))],
        out_specs=[rows(D), rows(D),
                   pl.BlockSpec((SUBLANES, tm), lambda i: (0, i)),
                   pl.BlockSpec((SUBLANES, tm), lambda i: (0, i))],
        out_shape=[jax.ShapeDtypeStruct((T, D), F32), jax.ShapeDtypeStruct((T, D), BF16),
                   jax.ShapeDtypeStruct((SUBLANES, T), jnp.int32),
                   jax.ShapeDtypeStruct((SUBLANES, T), F32)],
        compiler_params=_cparams(("parallel",)),
        name="outproj_ln_router",
    )(onsa, osb, ossm, x2d, w_out, gn, gs, lg, lb, rw_t, rb)


def _expert_kernel(be_ref, nu_ref, x_ref, wg_ref, wu_ref, wd_ref, o_ref):
    @pl.when(pl.program_id(0) < nu_ref[0])
    def _():
        x = x_ref[...]
        gate = _dot(x, wg_ref[...])
        up = _dot(x, wu_ref[...])
        hid = (gate * jax.nn.sigmoid(gate) * up).astype(BF16)
        o_ref[...] = _dot(hid, wd_ref[...]).astype(o_ref.dtype)

    @pl.when(pl.program_id(0) >= nu_ref[0])
    def _():
        o_ref[...] = jnp.zeros(o_ref.shape, o_ref.dtype)


def _expert_ffn(block_e, n_used, xs, wg, wu, wd):
    n_rows, D = xs.shape
    n_blocks = n_rows // EXPERT_BLOCK
    d_ff = wg.shape[-1]
    grid_spec = pltpu.PrefetchScalarGridSpec(
        num_scalar_prefetch=2,
        grid=(n_blocks,),
        in_specs=[pl.BlockSpec((EXPERT_BLOCK, D), lambda i, be, nu: (i, 0)),
                  pl.BlockSpec((None, D, d_ff), lambda i, be, nu: (be[i], 0, 0)),
                  pl.BlockSpec((None, D, d_ff), lambda i, be, nu: (be[i], 0, 0)),
                  pl.BlockSpec((None, d_ff, D), lambda i, be, nu: (be[i], 0, 0))],
        out_specs=pl.BlockSpec((EXPERT_BLOCK, D), lambda i, be, nu: (i, 0)),
    )
    return pl.pallas_call(
        _expert_kernel,
        grid_spec=grid_spec,
        out_shape=jax.ShapeDtypeStruct((n_rows, D), F32),
        compiler_params=_cparams(("arbitrary",)),
        name="expert_ffn",
    )(block_e, n_used, xs, wg, wu, wd)


def _ple_kernel(hb_ref, h_ref, p_ref, ffn_ref, wg_ref, wp_ref, lg_ref, lb_ref, o_ref, *, alpha):
    gate = jax.nn.sigmoid(_dot(hb_ref[...], wg_ref[...]))
    emb = _dot(p_ref[...].astype(BF16), wp_ref[...])
    v = alpha * h_ref[...] + ffn_ref[...] + gate * emb
    o_ref[...] = _layer_norm_rows(v, lg_ref[...], lb_ref[...])


def _ple_ln(hb, h, p2d, ffn, wg, wp, lg, lb, alpha):
    T, D = h.shape
    tm = min(MM_TM, T)
    P = p2d.shape[1]

    def rows(width):
        return pl.BlockSpec((tm, width), lambda i: (i, 0))

    def const(shape):
        return pl.BlockSpec(shape, lambda i: (0,) * len(shape))

    return pl.pallas_call(
        functools.partial(_ple_kernel, alpha=alpha),
        grid=(T // tm,),
        in_specs=[rows(D), rows(D), rows(P), rows(D), const((D, D)), const((P, D)),
                  const((1, D)), const((1, D))],
        out_specs=rows(D),
        out_shape=jax.ShapeDtypeStruct((T, D), F32),
        compiler_params=_cparams(("parallel",)),
        name="ple_ln",
    )(hb, h, p2d, ffn, wg, wp, lg, lb)


def _dispatch_plan(ids, T):
    TK = T * TOP_K
    flat_e = ids.T.reshape(TK)
    flat_tok = jnp.repeat(jnp.arange(T, dtype=jnp.int32), TOP_K)
    order = jnp.argsort(flat_e)
    se = flat_e[order]
    counts = jnp.zeros((N_EXPERTS,), jnp.int32).at[flat_e].add(1)
    padded = (counts + EXPERT_BLOCK - 1) // EXPERT_BLOCK * EXPERT_BLOCK
    start = jnp.cumsum(counts) - counts
    ends = jnp.cumsum(padded)
    pstart = ends - padded
    dest = pstart[se] + jnp.arange(TK, dtype=jnp.int32) - start[se]
    n_blocks = -(-TK // EXPERT_BLOCK) + N_EXPERTS
    n_rows = n_blocks * EXPERT_BLOCK
    row_tok = jnp.zeros((n_rows,), jnp.int32).at[dest].set(flat_tok[order])
    pos = jnp.zeros((TK,), jnp.int32).at[order].set(dest).reshape(T, TOP_K)
    block_e = jnp.minimum(jnp.searchsorted(ends, jnp.arange(n_blocks) * EXPERT_BLOCK, side='right'),
                          N_EXPERTS - 1).astype(jnp.int32)
    n_used = (ends[-1] // EXPERT_BLOCK).astype(jnp.int32).reshape(1)
    return row_tok, pos, block_e, n_used


def _prep_w_in(w_in_l):
    d_kv = 6 * NSA_KV_HEADS * HEAD_DIM
    offs = np.cumsum([0, D_NSA, d_kv, NSA_HEADS * 3, 3 * D_SB, D_SSM, CONV_CH, SSM_HEADS])
    q, kv, gate, sbqkv, z, xbc, dt = [w_in_l[:, offs[i]:offs[i + 1]] for i in range(7)]
    scale = HEAD_DIM ** -0.5
    main = jnp.concatenate([q * scale, kv, sbqkv[:, :D_SB] * scale, sbqkv[:, D_SB:], z, xbc], axis=1)
    pad = jnp.zeros((w_in_l.shape[0], LANES - NSA_HEADS * 3 - SSM_HEADS), w_in_l.dtype)
    misc = jnp.concatenate([gate, dt, pad], axis=1)
    return main.astype(BF16), misc.astype(BF16)


def _pad_lanes(v, offset):
    out = jnp.zeros((1, LANES), F32)
    return lax.dynamic_update_slice(out, v.reshape(1, -1).astype(F32), (0, offset))


def kernel(x, p, w_in, w_out, cmp_pe_k, cmp_w1_k, cmp_w2_k, cmp_pe_v, cmp_w1_v, cmp_w2_v, nsa_norm_g, sb_norm_g, conv_w, conv_b, dt_bias, a_log, d_skip, ssm_norm_g, ln1_g, ln1_b, ln2_g, ln2_b, router_w, router_b, expert_w_gate, expert_w_up, expert_w_down, ple_gate_w, ple_proj_w):
    B, S, D = x.shape
    depth = w_in.shape[0]
    T = B * S
    alpha = (2 * depth) ** 0.25
    NC = S // NSA_CMP_STRIDE
    x2d = x.reshape(T, D)
    rw_t = router_w.T
    rb = router_b.reshape(N_EXPERTS, 1)

    for i in range(depth):
        w_main, w_misc = _prep_w_in(w_in[i])
        proj, misc = _inproj(x2d, w_main, w_misc)

        raw = proj[:, COL_KV:COL_KV + 2 * LANES].reshape(B, S, 2, NSA_KV_HEADS, HEAD_DIM)
        xr = raw.transpose(2, 0, 3, 1, 4).reshape(2, B, NSA_KV_HEADS, NC, NSA_CMP_STRIDE * HEAD_DIM)
        pe = jnp.stack([cmp_pe_k[i], cmp_pe_v[i]]).reshape(2, 1, NSA_CMP_LEN * HEAD_DIM)
        pe = jnp.broadcast_to(pe, (2, SUBLANES, NSA_CMP_LEN * HEAD_DIM))
        cmp = _nsa_compress(xr, pe, jnp.stack([cmp_w1_k[i], cmp_w1_v[i]]),
                            jnp.stack([cmp_w2_k[i], cmp_w2_v[i]]))
        cmp = cmp.transpose(0, 1, 3, 2, 4).reshape(2, B, NC, LANES)
        o_nsa = _nsa_attention(proj, misc, cmp[0], cmp[1], B, S)

        o_sb = _sb_attention(proj, B, S)

        o_ssm = _ssd_mixer(
            proj, misc,
            jnp.concatenate([conv_w[i].reshape(SSM_CONV, CONV_CH),
                             jnp.zeros((SUBLANES - SSM_CONV, CONV_CH), F32)], axis=0),
            conv_b[i].reshape(1, CONV_CH), _pad_lanes(dt_bias[i], MISC_DT), _pad_lanes(a_log[i], MISC_DT),
            jnp.repeat(d_skip[i], SSM_HEAD_DIM).reshape(1, D_SSM), ssm_norm_g[i].reshape(1, D_SSM), B, S)

        h, hb, ids, wts = _outproj_ln_router(
            o_nsa, o_sb, o_ssm, x2d, w_out[i].astype(BF16), nsa_norm_g[i].reshape(1, D_NSA),
            sb_norm_g[i].reshape(1, D_SB), ln1_g[i].reshape(1, D), ln1_b[i].reshape(1, D), rw_t, rb, alpha)

        row_tok, pos, block_e, n_used = _dispatch_plan(ids[:TOP_K], T)
        xs = jnp.take(hb, row_tok, axis=0)
        ys = _expert_ffn(block_e, n_used, xs, expert_w_gate[i].astype(BF16), expert_w_up[i].astype(BF16),
                         expert_w_down[i].astype(BF16))
        w_tok = wts[:TOP_K].T
        ffn = (jnp.take(ys, pos[:, 0], axis=0) * w_tok[:, 0:1]
               + jnp.take(ys, pos[:, 1], axis=0) * w_tok[:, 1:2])

        x2d = _ple_ln(hb, h, p[i].reshape(T, -1), ffn, ple_gate_w[i].astype(BF16),
                      ple_proj_w[i].astype(BF16), ln2_g[i].reshape(1, D), ln2_b[i].reshape(1, D), alpha)

    return x2d.reshape(B, S, D)
```

```python
import functools
import math

import jax
import jax.numpy as jnp
import numpy as np
from jax import lax
from jax.experimental import pallas as pl
from jax.experimental.pallas import tpu as pltpu

F32 = jnp.float32
BF16 = jnp.bfloat16

HEAD_DIM = 64
NSA_HEADS = 12
NSA_KV_HEADS = 2
NSA_HPG = NSA_HEADS // NSA_KV_HEADS
NSA_CMP_LEN = 32
NSA_CMP_STRIDE = 16
NSA_CMP_HIDDEN = 256
NSA_SEL_LEN = 64
NSA_N_SEL = 16
NSA_WINDOW = 512
FORCE_BONUS = 1e4
MASKED_SCORE = -1e9
NEG_INF = -1e30
SB_HEADS = 8
SSM_HEADS = 12
SSM_HEAD_DIM = 64
SSM_GROUPS = 2
SSM_STATE = 128
SSM_CONV = 4
D_NSA = NSA_HEADS * HEAD_DIM
D_SB = SB_HEADS * HEAD_DIM
D_SSM = SSM_HEADS * SSM_HEAD_DIM
CONV_CH = D_SSM + 2 * SSM_GROUPS * SSM_STATE
N_EXPERTS = 32
N_EXPERT_GROUPS = 4
EXPERTS_PER_GROUP = N_EXPERTS // N_EXPERT_GROUPS
TOP_K = 2

LANES = 128
SUBLANES = 8
VMEM_LIMIT_BYTES = 56 * 1024 * 1024

COL_Q = 0
COL_KV = COL_Q + D_NSA
COL_SBQ = COL_KV + 6 * LANES
COL_SBK = COL_SBQ + D_SB
COL_SBV = COL_SBK + D_SB
COL_Z = COL_SBV + D_SB
COL_XBC = COL_Z + D_SSM
N_MAIN = COL_XBC + CONV_CH
MISC_GATE = 0
MISC_DT = NSA_HEADS * 3

QB = 128
SEL_CHUNK = 512
SB_TQ = 256
SSD_L = 128
MM_TM = 256
EXPERT_BLOCK = 256
SB_SKIP_BELOW = -150.0


def _cparams(sem):
    return pltpu.CompilerParams(dimension_semantics=sem, vmem_limit_bytes=VMEM_LIMIT_BYTES)


def _split3(x):
    hi = x.astype(BF16)
    r1 = x - hi.astype(F32)
    mid = r1.astype(BF16)
    lo = (r1 - mid.astype(F32)).astype(BF16)
    return hi, mid, lo


def _dot(a, b):
    return jnp.dot(a, b, preferred_element_type=F32)


def _dot_nt(a, b):
    return lax.dot_general(a, b, (((1,), (1,)), ((), ())), preferred_element_type=F32)


def _inproj_kernel(x_ref, w_ref, wm_ref, o_ref, m_ref, xb_ref):
    @pl.when(pl.program_id(1) == 0)
    def _():
        xb_ref[...] = x_ref[...].astype(BF16)
        m_ref[...] = _dot(xb_ref[...], wm_ref[...])

    o_ref[...] = _dot(xb_ref[...], w_ref[...]).astype(o_ref.dtype)


def _inproj(x2d, w_main, w_misc, tm=1024, tn=640):
    T, D = x2d.shape
    tm = min(tm, T)
    return pl.pallas_call(
        _inproj_kernel,
        grid=(T // tm, N_MAIN // tn),
        in_specs=[pl.BlockSpec((tm, D), lambda i, j: (i, 0)),
                  pl.BlockSpec((D, tn), lambda i, j: (0, j)),
                  pl.BlockSpec((D, LANES), lambda i, j: (0, 0))],
        out_specs=[pl.BlockSpec((tm, tn), lambda i, j: (i, j)),
                   pl.BlockSpec((tm, LANES), lambda i, j: (i, 0))],
        out_shape=[jax.ShapeDtypeStruct((T, N_MAIN), BF16),
                   jax.ShapeDtypeStruct((T, LANES), F32)],
        scratch_shapes=[pltpu.VMEM((tm, D), BF16)],
        compiler_params=_cparams(("parallel", "arbitrary")),
        name="inproj",
    )(x2d, w_main, w_misc)


def _gelu_tanh(x):
    return 0.5 * x * (1.0 + jnp.tanh(math.sqrt(2.0 / math.pi) * (x + 0.044715 * (x * x * x))))


def _cmp_kernel(x_ref, pe_ref, w1_ref, w2_ref, o_ref):
    nc = x_ref.shape[0]
    half = x_ref.shape[1]
    w1 = w1_ref[...].astype(BF16)
    pq = _dot(x_ref[...], w1[:half]), _dot(x_ref[...], w1[half:])
    pe_term = _dot(pe_ref[...].astype(BF16), w1)[0:1]
    h = pq[0] + pltpu.roll(pq[1], nc - 1, axis=0) + pe_term
    o_ref[...] = _dot(_gelu_tanh(h).astype(BF16), w2_ref[...].astype(BF16)).astype(o_ref.dtype)


def _nsa_compress(xr, pe, w1, w2):
    _, B, G, NC, W = xr.shape
    return pl.pallas_call(
        _cmp_kernel,
        grid=(2, B, G),
        in_specs=[pl.BlockSpec((None, None, None, NC, W), lambda s, b, g: (s, b, g, 0, 0)),
                  pl.BlockSpec((None, SUBLANES, 2 * W), lambda s, b, g: (s, 0, 0)),
                  pl.BlockSpec((None, 2 * W, NSA_CMP_HIDDEN), lambda s, b, g: (s, 0, 0)),
                  pl.BlockSpec((None, NSA_CMP_HIDDEN, HEAD_DIM), lambda s, b, g: (s, 0, 0))],
        out_specs=pl.BlockSpec((None, None, None, NC, HEAD_DIM), lambda s, b, g: (s, b, g, 0, 0)),
        out_shape=jax.ShapeDtypeStruct((2, B, G, NC, HEAD_DIM), BF16),
        compiler_params=_cparams(("parallel", "parallel", "parallel")),
        name="nsa_compress",
    )(xr, pe, w1, w2)


def _alibi_slopes():
    return [float(2.0 ** (-8.0 * (i + 1) / NSA_HEADS)) for i in range(NSA_HEADS)]


def _row_reduce(x, op, lane_reduce):
    n = x.shape[-1]
    acc = x[..., 0:LANES]
    for i in range(1, n // LANES):
        acc = op(acc, x[..., i * LANES:(i + 1) * LANES])
    return lane_reduce(acc, axis=-1, keepdims=True)


def _row_max(x):
    return _row_reduce(x, jnp.maximum, jnp.max)


def _row_sum(x):
    return _row_reduce(x, jnp.add, jnp.sum)


def _softmax_masked(lg):
    m = _row_max(lg)
    e = jnp.exp(lg - m)
    return e * (1.0 / _row_sum(e))


def _nsa_kernel(q_ref, misc_ref, kc_ref, vc_ref, ks_ref, vs_ref, kw_ref, vw_ref, ovt_ref, o_ref,
                m_sc, acc_sc, *, seq):
    H = NSA_HPG
    G = NSA_KV_HEADS
    R = H * QB
    nc = kc_ref.shape[0]
    nb = seq // NSA_SEL_LEN
    n_sel = min(NSA_N_SEL, nb)
    bpc = SEL_CHUNK // NSA_SEL_LEN
    q0 = pl.program_id(1) * QB
    t_q = q0 + lax.broadcasted_iota(jnp.int32, (QB, 1), 0)
    t3 = t_q[None]
    lane = lax.broadcasted_iota(jnp.int32, (QB, LANES), 1)
    gates = jax.nn.sigmoid(misc_ref[...])
    slopes = _alibi_slopes()

    def head_bias(g, pos):
        slope3 = jnp.concatenate(
            [jnp.full((1, 1, 1), slopes[g * H + h], F32) for h in range(H)], axis=0)
        return slope3 * pos.astype(F32)

    qps = []
    for g in range(G):
        in_g = (lane >= g * HEAD_DIM) & (lane < (g + 1) * HEAD_DIM)
        parts = []
        for h in range(H):
            hh = g * H + h
            slab = q_ref[:, (hh // 2) * LANES:(hh // 2 + 1) * LANES].astype(F32)
            if hh % 2 != g:
                slab = pltpu.roll(slab, HEAD_DIM, axis=1)
            parts.append(jnp.where(in_g, slab, 0.0).astype(BF16))
        qps.append(jnp.concatenate(parts, axis=0))

    cmp_end = lax.broadcasted_iota(jnp.int32, (1, 1, nc), 2) * NSA_CMP_STRIDE + (NSA_CMP_LEN - 1)
    neg_c = jnp.where(t3 >= cmp_end, 0.0, NEG_INF)
    row_ok = (t_q >= NSA_CMP_LEN - 1).astype(F32)
    t_row = q0 + lax.broadcasted_iota(jnp.int32, (1, QB), 1)
    blk_t = lax.broadcasted_iota(jnp.int32, (nb, QB), 0)
    cur_t = t_row // NSA_SEL_LEN
    forced = (blk_t == 0) | (blk_t == cur_t) | (blk_t == cur_t - 1)
    free = (blk_t <= cur_t) & jnp.logical_not(forced)
    o_cs, sels_t = [], []
    for g in range(G):
        lc = _dot_nt(qps[g], kc_ref[...]).reshape(H, QB, nc)
        p_c = _softmax_masked(lc + head_bias(g, cmp_end) + neg_c) * row_ok[None]
        o_cs.append(_dot(p_c.reshape(R, nc).astype(BF16), vc_ref[...]))
        p_sum = jnp.sum(p_c, axis=0)
        imp_t = sum(_dot_nt(ovt_ref[...], piece) for piece in _split3(p_sum))
        score = jnp.where(free, imp_t, -3e38)
        sel_t = forced.astype(F32)
        for _ in range(n_sel - 3):
            mx = jnp.max(score, axis=0, keepdims=True)
            first = jnp.min(jnp.where(score == mx, blk_t, nb), axis=0, keepdims=True)
            pick = blk_t == first
            sel_t = jnp.where(pick, 1.0, sel_t)
            score = jnp.where(pick, -3e38, score)
        sels_t.append(sel_t)
    sel_bf = [s.T.astype(BF16) for s in sels_t]

    m_sc[...] = jnp.full(m_sc.shape, NEG_INF, F32)
    acc_sc[...] = jnp.zeros(acc_sc.shape, F32)
    n_chunks = (q0 + QB + SEL_CHUNK - 1) // SEL_CHUNK
    lane_row = lax.broadcasted_iota(jnp.int32, (1, LANES), 1)
    own_lanes = [(lane_row >= g * HEAD_DIM) & (lane_row < (g + 1) * HEAD_DIM) for g in range(G)]

    def with_ones(v, g):
        return jnp.where(own_lanes[g], v, jnp.ones((), v.dtype))

    def exp_tiles(lg, m_rep):
        n = lg.shape[-1] // LANES
        return jnp.concatenate([jnp.exp(lg[..., i * LANES:(i + 1) * LANES] - m_rep) for i in range(n)], axis=-1)

    def sel_step(i, carry):
        c = n_chunks - 1 - i
        k0 = pl.multiple_of(c * SEL_CHUNK, SEL_CHUNK)
        in_chunk = (blk_t // bpc) == c
        s_pos = k0 + lax.broadcasted_iota(jnp.int32, (1, 1, SEL_CHUNK), 2)
        causal = t3 >= s_pos
        ej = lax.broadcasted_iota(jnp.int32, (nb, SEL_CHUNK), 0)
        es = lax.broadcasted_iota(jnp.int32, (nb, SEL_CHUNK), 1)
        expand = (ej == c * bpc + es // NSA_SEL_LEN).astype(BF16)
        for g in range(G):
            hit = jnp.max(jnp.where(in_chunk, sels_t[g], 0.0))

            @pl.when(hit > 0.0)
            def _():
                key_sel = _dot(sel_bf[g], expand)
                neg = jnp.where((key_sel[None] > 0.5) & causal, 0.0, NEG_INF)
                ls = _dot_nt(qps[g], ks_ref[pl.ds(k0, SEL_CHUNK), :]).reshape(H, QB, SEL_CHUNK)
                lg = ls + head_bias(g, s_pos) + neg
                m_old = m_sc[g].reshape(H, QB, LANES)
                m_new = jnp.maximum(m_old, _row_max(lg))
                alpha = jnp.exp(m_old - m_new)
                pr = exp_tiles(lg, m_new)
                pv = _dot(pr.reshape(R, SEL_CHUNK).astype(BF16), with_ones(vs_ref[pl.ds(k0, SEL_CHUNK), :], g))
                acc_sc[g] = alpha.reshape(R, LANES) * acc_sc[g] + pv
                m_sc[g] = m_new.reshape(R, LANES)

        return carry

    lax.fori_loop(0, n_chunks, sel_step, 0)

    wk = NSA_WINDOW + QB
    w0 = pl.multiple_of(jnp.maximum(q0 - NSA_WINDOW, 0), QB)
    w_pos = w0 + lax.broadcasted_iota(jnp.int32, (1, 1, wk), 2)
    dist_w = t3 - w_pos
    neg_w = jnp.where((dist_w >= 0) & (dist_w < NSA_WINDOW), 0.0, NEG_INF)

    for g in range(G):
        acc = acc_sc[g]
        o_s = acc / pltpu.roll(acc, HEAD_DIM, axis=1)
        lw = _dot_nt(qps[g], kw_ref[pl.ds(w0, wk), :]).reshape(H, QB, wk) + head_bias(g, w_pos) + neg_w
        e_w = jnp.exp(lw - _row_max(lw))
        ow = _dot(e_w.reshape(R, wk).astype(BF16), with_ones(vw_ref[pl.ds(w0, wk), :], g))
        o_w = ow / pltpu.roll(ow, HEAD_DIM, axis=1)
        o_c = o_cs[g]

        outs = []
        for h in range(H):
            hh = g * H + h
            rows = slice(h * QB, (h + 1) * QB)
            gc = gates[:, hh * 3 + 0:hh * 3 + 1]
            gs = gates[:, hh * 3 + 1:hh * 3 + 2]
            gw = gates[:, hh * 3 + 2:hh * 3 + 3]
            outs.append(gc * o_c[rows] + gs * o_s[rows] + gw * o_w[rows])
        for pair in range(H // 2):
            a, b = outs[2 * pair], outs[2 * pair + 1]
            if g == 0:
                slab = jnp.where(lane < HEAD_DIM, a, pltpu.roll(b, HEAD_DIM, axis=1))
            else:
                slab = jnp.where(lane < HEAD_DIM, pltpu.roll(a, HEAD_DIM, axis=1), b)
            col = (g * H // 2 + pair) * LANES
            o_ref[:, col:col + LANES] = slab.astype(o_ref.dtype)


def _overlap_matrix_t(nc, nb):
    k = np.arange(nc)[:, None]
    j = np.arange(nb)[None, :]
    ratio = NSA_SEL_LEN // NSA_CMP_STRIDE
    ov = (k < ratio * j + ratio) & (k + NSA_CMP_LEN // NSA_CMP_STRIDE > ratio * j) & (k < nc - 1)
    return jnp.asarray(ov.T.astype(np.float32), dtype=BF16)


def _nsa_attention(proj, misc, kcmp, vcmp, B, S):
    T = B * S
    nqb = S // QB
    NC = kcmp.shape[1]
    nb = S // NSA_SEL_LEN
    assert min(NSA_N_SEL, nb) > 3 and S >= NSA_WINDOW + QB
    R = NSA_HPG * QB
    G = NSA_KV_HEADS
    kv0 = COL_KV // LANES

    def slab(j):
        return pl.BlockSpec((S, LANES), lambda b, i, j=j: (b, kv0 + j))

    return pl.pallas_call(
        functools.partial(_nsa_kernel, seq=S),
        grid=(B, nqb),
        in_specs=[pl.BlockSpec((QB, D_NSA), lambda b, i: (b * nqb + i, 0)),
                  pl.BlockSpec((QB, LANES), lambda b, i: (b * nqb + i, 0)),
                  pl.BlockSpec((None, NC, LANES), lambda b, i: (b, 0, 0)),
                  pl.BlockSpec((None, NC, LANES), lambda b, i: (b, 0, 0)),
                  slab(2), slab(3), slab(4), slab(5),
                  pl.BlockSpec((nb, NC), lambda b, i: (0, 0))],
        out_specs=pl.BlockSpec((QB, D_NSA), lambda b, i: (b * nqb + i, 0)),
        out_shape=jax.ShapeDtypeStruct((T, D_NSA), F32),
        scratch_shapes=[pltpu.VMEM((G, R, LANES), F32), pltpu.VMEM((G, R, LANES), F32)],
        compiler_params=_cparams(("parallel", "arbitrary")),
        name="nsa_attention",
    )(proj, misc, kcmp, vcmp, proj, proj, proj, proj, _overlap_matrix_t(NC, nb))


def _softplus(z):
    return jnp.maximum(z, 0.0) + jnp.log(1.0 + jnp.exp(-jnp.abs(z)))


def _sb_kernel(q_ref, k_ref, v_ref, o_ref, c_sc, acc_sc):
    tq = q_ref.shape[0]
    qi = pl.program_id(2)
    lane = lax.broadcasted_iota(jnp.int32, (tq, LANES), 1)
    jr = lax.broadcasted_iota(jnp.int32, (tq, tq), 0)
    sc = lax.broadcasted_iota(jnp.int32, (tq, tq), 1)
    upper = (jr > sc).astype(BF16)
    diag_mask = sc < jr
    qf = q_ref[...].astype(F32)
    res = []
    for h in range(2):
        in_h = (lane >= h * HEAD_DIM) & (lane < (h + 1) * HEAD_DIM)
        qp = jnp.where(in_h, qf, 0.0).astype(BF16)
        c_sc[...] = jnp.zeros(c_sc.shape, F32)
        acc_sc[...] = jnp.zeros(acc_sc.shape, F32)

        def block(kb, mask):
            k0 = pl.multiple_of(kb * tq, tq)
            z = _dot_nt(qp, k_ref[pl.ds(k0, tq), :])
            sp = _softplus(z)
            log1m = -sp if mask is None else jnp.where(mask, -sp, 0.0)
            hi, mid, lo = _split3(log1m)
            tail = _dot(hi, upper) + _dot(mid, upper) + _dot(lo, upper)
            arg = (z - sp) + tail + c_sc[...]
            a = jnp.exp(arg)
            if mask is not None:
                a = jnp.where(mask, a, 0.0)
            acc_sc[...] += _dot(a.astype(BF16), v_ref[pl.ds(k0, tq), :])
            c_sc[...] += jnp.sum(log1m, axis=-1, keepdims=True)

        block(qi, diag_mask)

        def cond(st):
            kb, cmax = st
            return (kb >= 0) & (cmax > SB_SKIP_BELOW)

        def body(st):
            kb, _ = st
            block(kb, None)
            return kb - 1, jnp.max(c_sc[...])

        lax.while_loop(cond, body, (qi - 1, jnp.max(c_sc[...])))
        res.append(acc_sc[...])
    o_ref[...] = jnp.where(lane < HEAD_DIM, res[0], res[1]).astype(o_ref.dtype)


def _sb_attention(proj, B, S):
    T = B * S
    tq = min(SB_TQ, S)
    nq = S // tq
    nslab = D_SB // LANES
    return pl.pallas_call(
        _sb_kernel,
        grid=(B, nslab, nq),
        in_specs=[pl.BlockSpec((tq, LANES), lambda b, s, i: (b * nq + i, COL_SBQ // LANES + s)),
                  pl.BlockSpec((S, LANES), lambda b, s, i: (b, COL_SBK // LANES + s)),
                  pl.BlockSpec((S, LANES), lambda b, s, i: (b, COL_SBV // LANES + s))],
        out_specs=pl.BlockSpec((tq, LANES), lambda b, s, i: (b * nq + i, s)),
        out_shape=jax.ShapeDtypeStruct((T, D_SB), F32),
        scratch_shapes=[pltpu.VMEM((tq, 1), F32), pltpu.VMEM((tq, LANES), F32)],
        compiler_params=_cparams(("parallel", "parallel", "arbitrary")),
        name="sb_attention",
    )(proj, proj, proj)


def _ssd_kernel(z_ref, xbc_ref, misc_ref, cw_ref, cb_ref, dtb_ref, alog_ref, dskip_ref, g_ref,
                o_ref, xbuf, state):
    L = SSD_L
    c = pl.program_id(1)

    @pl.when(c == 0)
    def _():
        xbuf[0:SUBLANES, :] = jnp.zeros((SUBLANES, CONV_CH), F32)
        state[...] = jnp.zeros(state.shape, F32)

    xbuf[SUBLANES:SUBLANES + L, :] = xbc_ref[...].astype(F32)
    conv = cb_ref[...]
    for w in range(SSM_CONV):
        off = SUBLANES - (SSM_CONV - 1) + w
        conv = conv + xbuf[off:off + L, :] * cw_ref[w:w + 1, :]
    xbuf[0:SUBLANES, :] = xbuf[L:L + SUBLANES, :]
    xbc = conv * jax.nn.sigmoid(conv)
    xs = xbc[:, :D_SSM]
    b_in = [xbc[:, D_SSM + g * SSM_STATE:D_SSM + (g + 1) * SSM_STATE] for g in range(SSM_GROUPS)]
    c0 = D_SSM + SSM_GROUPS * SSM_STATE
    c_in = [xbc[:, c0 + g * SSM_STATE:c0 + (g + 1) * SSM_STATE].astype(BF16) for g in range(SSM_GROUPS)]

    dt = _softplus(misc_ref[...] + dtb_ref[...])
    a_neg = -jnp.exp(alog_ref[...])
    d_a = dt * a_neg
    rr = lax.broadcasted_iota(jnp.int32, (L, L), 0)
    cc = lax.broadcasted_iota(jnp.int32, (L, L), 1)
    causal = cc <= rr
    tril = causal.astype(BF16)
    a_cum = sum(_dot(tril, piece) for piece in _split3(d_a))
    a_cum_t = a_cum.T
    lane = lax.broadcasted_iota(jnp.int32, (L, LANES), 1)
    b_t = [b.T.astype(F32) for b in b_in]
    cb = [_dot_nt(c_in[g], b_in[g].astype(BF16)) for g in range(SSM_GROUPS)]

    ys = []
    for pair in range(SSM_HEADS // 2):
        h0, h1 = 2 * pair, 2 * pair + 1
        dt_pair = jnp.where(lane < SSM_HEAD_DIM, dt[:, MISC_DT + h0:MISC_DT + h0 + 1],
                            dt[:, MISC_DT + h1:MISC_DT + h1 + 1])
        xs_pair = xs[:, pair * LANES:(pair + 1) * LANES]
        xdt = (xs_pair * dt_pair).astype(BF16)
        y_pair = []
        for h in (h0, h1):
            g = h // (SSM_HEADS // SSM_GROUPS)
            col = a_cum[:, MISC_DT + h:MISC_DT + h + 1]
            row = a_cum_t[MISC_DT + h:MISC_DT + h + 1, :]
            last = row[:, L - 1:L]
            decay = jnp.exp(jnp.where(causal, col - row, NEG_INF))
            y = _dot((cb[g] * decay).astype(BF16), xdt)
            y = y + _dot(c_in[g], state[h].astype(BF16)) * jnp.exp(col)
            new_state = state[h] * jnp.exp(last) + _dot((b_t[g] * jnp.exp(last - row)).astype(BF16), xdt)
            state[h] = new_state
            y_pair.append(y)
        ys.append(jnp.where(lane < SSM_HEAD_DIM, y_pair[0], y_pair[1]))
    y = jnp.concatenate(ys, axis=1) + dskip_ref[...] * xs
    zf = z_ref[...].astype(F32)
    y = y * (zf * jax.nn.sigmoid(zf))
    y = y * lax.rsqrt(jnp.mean(y * y, axis=-1, keepdims=True) + 1e-6) * g_ref[...]
    o_ref[...] = y.astype(o_ref.dtype)


def _ssd_mixer(proj, misc, conv_w, conv_b, dtb_pad, alog_pad, dskip_full, norm_g, B, S):
    T = B * S
    nc = S // SSD_L

    def row(width):
        return pl.BlockSpec((1, width), lambda b, c: (0, 0))

    return pl.pallas_call(
        _ssd_kernel,
        grid=(B, nc),
        in_specs=[pl.BlockSpec((SSD_L, D_SSM), lambda b, c: (b * nc + c, COL_Z // D_SSM)),
                  pl.BlockSpec((SSD_L, CONV_CH), lambda b, c: (b * nc + c, COL_XBC // CONV_CH)),
                  pl.BlockSpec((SSD_L, LANES), lambda b, c: (b * nc + c, 0)),
                  pl.BlockSpec((SUBLANES, CONV_CH), lambda b, c: (0, 0)),
                  row(CONV_CH), row(LANES), row(LANES), row(D_SSM), row(D_SSM)],
        out_specs=pl.BlockSpec((SSD_L, D_SSM), lambda b, c: (b * nc + c, 0)),
        out_shape=jax.ShapeDtypeStruct((T, D_SSM), F32),
        scratch_shapes=[pltpu.VMEM((SSD_L + SUBLANES, CONV_CH), F32),
                        pltpu.VMEM((SSM_HEADS, SSM_STATE, LANES), F32)],
        compiler_params=_cparams(("parallel", "arbitrary")),
        name="ssd_mixer",
    )(proj, proj, misc, conv_w, conv_b, dtb_pad, alog_pad, dskip_full, norm_g)


def _layer_norm_rows(v, g, b):
    mu = jnp.mean(v, axis=-1, keepdims=True)
    d = v - mu
    var = jnp.mean(d * d, axis=-1, keepdims=True)
    return d * lax.rsqrt(var + 1e-5) * g + b


def _rms_rows(v, g):
    return v * lax.rsqrt(jnp.mean(v * v, axis=-1, keepdims=True) + 1e-6) * g


def _route(logits_t):
    tm = logits_t.shape[1]
    m = jnp.max(logits_t, axis=0, keepdims=True)
    e = jnp.exp(logits_t - m)
    probs = e / jnp.sum(e, axis=0, keepdims=True)
    p3 = probs.reshape(N_EXPERT_GROUPS, EXPERTS_PER_GROUP, tm)
    io_e = lax.broadcasted_iota(jnp.int32, p3.shape, 1)

    def top2(p, io, n):
        m1 = jnp.max(p, axis=-2, keepdims=True)
        i1 = jnp.min(jnp.where(p == m1, io, n), axis=-2, keepdims=True)
        rest = jnp.where(io == i1, -1.0, p)
        m2 = jnp.max(rest, axis=-2, keepdims=True)
        i2 = jnp.min(jnp.where(rest == m2, io, n), axis=-2, keepdims=True)
        return m1, i1, m2, i2

    m1, _, m2, _ = top2(p3, io_e, EXPERTS_PER_GROUP)
    gscore = (m1 + m2)[:, 0, :]
    io_g = lax.broadcasted_iota(jnp.int32, gscore.shape, 0)
    gmax = jnp.max(gscore, axis=0, keepdims=True)
    gsel = jnp.min(jnp.where(gscore == gmax, io_g, N_EXPERT_GROUPS), axis=0, keepdims=True)
    in_group = jnp.sum(jnp.where(io_g[:, None, :] == gsel[None], p3, 0.0), axis=0)
    io_l = lax.broadcasted_iota(jnp.int32, in_group.shape, 0)
    w1, e1, w2, e2 = top2(in_group, io_l, EXPERTS_PER_GROUP)
    wsum = w1 + w2
    ids = jnp.concatenate([gsel * EXPERTS_PER_GROUP + e1, gsel * EXPERTS_PER_GROUP + e2], axis=0)
    wts = jnp.concatenate([w1 / wsum, w2 / wsum], axis=0)
    return ids, wts


def _outproj_kernel(onsa_ref, osb_ref, ossm_ref, x_ref, w_ref, gn_ref, gs_ref, lg_ref, lb_ref,
                    rw_ref, rb_ref, h_ref, hb_ref, ids_ref, wts_ref, *, alpha):
    on = _rms_rows(onsa_ref[...], gn_ref[...]).astype(BF16)
    os_ = _rms_rows(osb_ref[...], gs_ref[...]).astype(BF16)
    om = ossm_ref[...].astype(BF16)
    mix = (_dot(on, w_ref[0:D_NSA, :]) + _dot(os_, w_ref[D_NSA:D_NSA + D_SB, :])
           + _dot(om, w_ref[D_NSA + D_SB:, :]))
    h = _layer_norm_rows(alpha * x_ref[...] + mix, lg_ref[...], lb_ref[...])
    h_ref[...] = h
    hb_ref[...] = h.astype(BF16)
    h_hi, h_mid, _ = _split3(h)
    r_hi, r_mid, _ = _split3(rw_ref[...])
    logits_t = (_dot_nt(r_hi, h_hi) + _dot_nt(r_hi, h_mid) + _dot_nt(r_mid, h_hi)) + rb_ref[...]
    ids, wts = _route(logits_t)
    pad_i = jnp.zeros((SUBLANES - TOP_K, ids.shape[1]), jnp.int32)
    ids_ref[...] = jnp.concatenate([ids, pad_i], axis=0)
    wts_ref[...] = jnp.concatenate([wts, pad_i.astype(F32)], axis=0)


def _outproj_ln_router(onsa, osb, ossm, x2d, w_out, gn, gs, lg, lb, rw_t, rb, alpha):
    T, D = x2d.shape
    tm = min(MM_TM, T)

    def rows(width):
        return pl.BlockSpec((tm, width), lambda i: (i, 0))

    def const(shape):
        return pl.BlockSpec(shape, lambda i: (0,) * len(shape))

    return pl.pallas_call(
        functools.partial(_outproj_kernel, alpha=alpha),
        grid=(T // tm,),
        in_specs=[rows(D_NSA), rows(D_SB), rows(D_SSM), rows(D),
                  const((D, D)), const((1, D_NSA)), const((1, D_SB)), const((1, D)), const((1, D)),
                  const((N_EXPERTS, D)), const((N_EXPERTS, 1))],
        out_specs=[rows(D), rows(D),
                   pl.BlockSpec((SUBLANES, tm), lambda i: (0, i)),
                   pl.BlockSpec((SUBLANES, tm), lambda i: (0, i))],
        out_shape=[jax.ShapeDtypeStruct((T, D), F32), jax.ShapeDtypeStruct((T, D), BF16),
                   jax.ShapeDtypeStruct((SUBLANES, T), jnp.int32),
                   jax.ShapeDtypeStruct((SUBLANES, T), F32)],
        compiler_params=_cparams(("parallel",)),
        name="outproj_ln_router",
    )(onsa, osb, ossm, x2d, w_out, gn, gs, lg, lb, rw_t, rb)


def _expert_kernel(be_ref, nu_ref, x_ref, wg_ref, wu_ref, wd_ref, o_ref):
    @pl.when(pl.program_id(0) < nu_ref[0])
    def _():
        x = x_ref[...]
        gate = _dot(x, wg_ref[...])
        up = _dot(x, wu_ref[...])
        hid = (gate * jax.nn.sigmoid(gate) * up).astype(BF16)
        o_ref[...] = _dot(hid, wd_ref[...]).astype(o_ref.dtype)

    @pl.when(pl.program_id(0) >= nu_ref[0])
    def _():
        o_ref[...] = jnp.zeros(o_ref.shape, o_ref.dtype)


def _expert_ffn(block_e, n_used, xs, wg, wu, wd):
    n_rows, D = xs.shape
    n_blocks = n_rows // EXPERT_BLOCK
    d_ff = wg.shape[-1]
    grid_spec = pltpu.PrefetchScalarGridSpec(
        num_scalar_prefetch=2,
        grid=(n_blocks,),
        in_specs=[pl.BlockSpec((EXPERT_BLOCK, D), lambda i, be, nu: (i, 0)),
                  pl.BlockSpec((None, D, d_ff), lambda i, be, nu: (be[i], 0, 0)),
                  pl.BlockSpec((None, D, d_ff), lambda i, be, nu: (be[i], 0, 0)),
                  pl.BlockSpec((None, d_ff, D), lambda i, be, nu: (be[i], 0, 0))],
        out_specs=pl.BlockSpec((EXPERT_BLOCK, D), lambda i, be, nu: (i, 0)),
    )
    return pl.pallas_call(
        _expert_kernel,
        grid_spec=grid_spec,
        out_shape=jax.ShapeDtypeStruct((n_rows, D), F32),
        compiler_params=_cparams(("arbitrary",)),
        name="expert_ffn",
    )(block_e, n_used, xs, wg, wu, wd)


def _ple_kernel(hb_ref, h_ref, p_ref, ffn_ref, wg_ref, wp_ref, lg_ref, lb_ref, o_ref, *, alpha):
    gate = jax.nn.sigmoid(_dot(hb_ref[...], wg_ref[...]))
    emb = _dot(p_ref[...].astype(BF16), wp_ref[...])
    v = alpha * h_ref[...] + ffn_ref[...] + gate * emb
    o_ref[...] = _layer_norm_rows(v, lg_ref[...], lb_ref[...])


def _ple_ln(hb, h, p2d, ffn, wg, wp, lg, lb, alpha):
    T, D = h.shape
    tm = min(MM_TM, T)
    P = p2d.shape[1]

    def rows(width):
        return pl.BlockSpec((tm, width), lambda i: (i, 0))

    def const(shape):
        return pl.BlockSpec(shape, lambda i: (0,) * len(shape))

    return pl.pallas_call(
        functools.partial(_ple_kernel, alpha=alpha),
        grid=(T // tm,),
        in_specs=[rows(D), rows(D), rows(P), rows(D), const((D, D)), const((P, D)),
                  const((1, D)), const((1, D))],
        out_specs=rows(D),
        out_shape=jax.ShapeDtypeStruct((T, D), F32),
        compiler_params=_cparams(("parallel",)),
        name="ple_ln",
    )(hb, h, p2d, ffn, wg, wp, lg, lb)


def _dispatch_plan(ids, T):
    TK = T * TOP_K
    flat_e = ids.T.reshape(TK)
    flat_tok = jnp.repeat(jnp.arange(T, dtype=jnp.int32), TOP_K)
    order = jnp.argsort(flat_e)
    se = flat_e[order]
    counts = jnp.zeros((N_EXPERTS,), jnp.int32).at[flat_e].add(1)
    padded = (counts + EXPERT_BLOCK - 1) // EXPERT_BLOCK * EXPERT_BLOCK
    start = jnp.cumsum(counts) - counts
    ends = jnp.cumsum(padded)
    pstart = ends - padded
    dest = pstart[se] + jnp.arange(TK, dtype=jnp.int32) - start[se]
    n_blocks = -(-TK // EXPERT_BLOCK) + N_EXPERTS
    n_rows = n_blocks * EXPERT_BLOCK
    row_tok = jnp.zeros((n_rows,), jnp.int32).at[dest].set(flat_tok[order])
    pos = jnp.zeros((TK,), jnp.int32).at[order].set(dest).reshape(T, TOP_K)
    block_e = jnp.minimum(jnp.searchsorted(ends, jnp.arange(n_blocks) * EXPERT_BLOCK, side='right'),
                          N_EXPERTS - 1).astype(jnp.int32)
    n_used = (ends[-1] // EXPERT_BLOCK).astype(jnp.int32).reshape(1)
    return row_tok, pos, block_e, n_used


def _prep_w_in(w_in_l):
    d_kv = 6 * NSA_KV_HEADS * HEAD_DIM
    offs = np.cumsum([0, D_NSA, d_kv, NSA_HEADS * 3, 3 * D_SB, D_SSM, CONV_CH, SSM_HEADS])
    q, kv, gate, sbqkv, z, xbc, dt = [w_in_l[:, offs[i]:offs[i + 1]] for i in range(7)]
    scale = HEAD_DIM ** -0.5
    main = jnp.concatenate([q * scale, kv, sbqkv[:, :D_SB] * scale, sbqkv[:, D_SB:], z, xbc], axis=1)
    pad = jnp.zeros((w_in_l.shape[0], LANES - NSA_HEADS * 3 - SSM_HEADS), w_in_l.dtype)
    misc = jnp.concatenate([gate, dt, pad], axis=1)
    return main.astype(BF16), misc.astype(BF16)


def _pad_lanes(v, offset):
    out = jnp.zeros((1, LANES), F32)
    return lax.dynamic_update_slice(out, v.reshape(1, -1).astype(F32), (0, offset))


def kernel(x, p, w_in, w_out, cmp_pe_k, cmp_w1_k, cmp_w2_k, cmp_pe_v, cmp_w1_v, cmp_w2_v, nsa_norm_g, sb_norm_g, conv_w, conv_b, dt_bias, a_log, d_skip, ssm_norm_g, ln1_g, ln1_b, ln2_g, ln2_b, router_w, router_b, expert_w_gate, expert_w_up, expert_w_down, ple_gate_w, ple_proj_w):
    B, S, D = x.shape
    depth = w_in.shape[0]
    T = B * S
    alpha = (2 * depth) ** 0.25
    NC = S // NSA_CMP_STRIDE
    x2d = x.reshape(T, D)
    rw_t = router_w.T
    rb = router_b.reshape(N_EXPERTS, 1)

    for i in range(depth):
        w_main, w_misc = _prep_w_in(w_in[i])
        proj, misc = _inproj(x2d, w_main, w_misc)

        raw = proj[:, COL_KV:COL_KV + 2 * LANES].reshape(B, S, 2, NSA_KV_HEADS, HEAD_DIM)
        xr = raw.transpose(2, 0, 3, 1, 4).reshape(2, B, NSA_KV_HEADS, NC, NSA_CMP_STRIDE * HEAD_DIM)
        pe = jnp.stack([cmp_pe_k[i], cmp_pe_v[i]]).reshape(2, 1, NSA_CMP_LEN * HEAD_DIM)
        pe = jnp.broadcast_to(pe, (2, SUBLANES, NSA_CMP_LEN * HEAD_DIM))
        cmp = _nsa_compress(xr, pe, jnp.stack([cmp_w1_k[i], cmp_w1_v[i]]),
                            jnp.stack([cmp_w2_k[i], cmp_w2_v[i]]))
        cmp = cmp.transpose(0, 1, 3, 2, 4).reshape(2, B, NC, LANES)
        o_nsa = _nsa_attention(proj, misc, cmp[0], cmp[1], B, S)

        o_sb = _sb_attention(proj, B, S)

        o_ssm = _ssd_mixer(
            proj, misc,
            jnp.concatenate([conv_w[i].reshape(SSM_CONV, CONV_CH),
                             jnp.zeros((SUBLANES - SSM_CONV, CONV_CH), F32)], axis=0),
            conv_b[i].reshape(1, CONV_CH), _pad_lanes(dt_bias[i], MISC_DT), _pad_lanes(a_log[i], MISC_DT),
            jnp.repeat(d_skip[i], SSM_HEAD_DIM).reshape(1, D_SSM), ssm_norm_g[i].reshape(1, D_SSM), B, S)

        h, hb, ids, wts = _outproj_ln_router(
            o_nsa, o_sb, o_ssm, x2d, w_out[i].astype(BF16), nsa_norm_g[i].reshape(1, D_NSA),
            sb_norm_g[i].reshape(1, D_SB), ln1_g[i].reshape(1, D), ln1_b[i].reshape(1, D), rw_t, rb, alpha)

        row_tok, pos, block_e, n_used = _dispatch_plan(ids[:TOP_K], T)
        xs = jnp.take(hb, row_tok, axis=0)
        ys = _expert_ffn(block_e, n_used, xs, expert_w_gate[i].astype(BF16), expert_w_up[i].astype(BF16),
                         expert_w_down[i].astype(BF16))
        w_tok = wts[:TOP_K].T
        ffn = (jnp.take(ys, pos[:, 0], axis=0) * w_tok[:, 0:1]
               + jnp.take(ys, pos[:, 1], axis=0) * w_tok[:, 1:2])

        x2d = _ple_ln(hb, h, p[i].reshape(T, -1), ffn, ple_gate_w[i].astype(BF16),
                      ple_proj_w[i].astype(BF16), ln2_g[i].reshape(1, D), ln2_b[i].reshape(1, D), alpha)

    return x2d.reshape(B, S, D)
```

```python
import functools
import math

import jax
import jax.numpy as jnp
import numpy as np
from jax import lax
from jax.experimental import pallas as pl
from jax.experimental.pallas import tpu as pltpu

F32 = jnp.float32
BF16 = jnp.bfloat16

HEAD_DIM = 64
NSA_HEADS = 12
NSA_KV_HEADS = 2
NSA_HPG = NSA_HEADS // NSA_KV_HEADS
NSA_CMP_LEN = 32
NSA_CMP_STRIDE = 16
NSA_CMP_HIDDEN = 256
NSA_SEL_LEN = 64
NSA_N_SEL = 16
NSA_WINDOW = 512
FORCE_BONUS = 1e4
MASKED_SCORE = -1e9
NEG_INF = -1e30
SB_HEADS = 8
SSM_HEADS = 12
SSM_HEAD_DIM = 64
SSM_GROUPS = 2
SSM_STATE = 128
SSM_CONV = 4
D_NSA = NSA_HEADS * HEAD_DIM
D_SB = SB_HEADS * HEAD_DIM
D_SSM = SSM_HEADS * SSM_HEAD_DIM
CONV_CH = D_SSM + 2 * SSM_GROUPS * SSM_STATE
N_EXPERTS = 32
N_EXPERT_GROUPS = 4
EXPERTS_PER_GROUP = N_EXPERTS // N_EXPERT_GROUPS
TOP_K = 2

LANES = 128
SUBLANES = 8
VMEM_LIMIT_BYTES = 56 * 1024 * 1024

COL_Q = 0
COL_KV = COL_Q + D_NSA
COL_SBQ = COL_KV + 6 * LANES
COL_SBK = COL_SBQ + D_SB
COL_SBV = COL_SBK + D_SB
COL_Z = COL_SBV + D_SB
COL_XBC = COL_Z + D_SSM
N_MAIN = COL_XBC + CONV_CH
MISC_GATE = 0
MISC_DT = NSA_HEADS * 3

QB = 128
SEL_CHUNK = 512
SB_TQ = 256
SSD_L = 128
MM_TM = 256
EXPERT_BLOCK = 256
SB_SKIP_BELOW = -150.0


def _cparams(sem):
    return pltpu.CompilerParams(dimension_semantics=sem, vmem_limit_bytes=VMEM_LIMIT_BYTES)


def _split3(x):
    hi = x.astype(BF16)
    r1 = x - hi.astype(F32)
    mid = r1.astype(BF16)
    lo = (r1 - mid.astype(F32)).astype(BF16)
    return hi, mid, lo


def _dot(a, b):
    return jnp.dot(a, b, preferred_element_type=F32)


def _dot_nt(a, b):
    return lax.dot_general(a, b, (((1,), (1,)), ((), ())), preferred_element_type=F32)


def _inproj_kernel(x_ref, w_ref, wm_ref, o_ref, m_ref, xb_ref):
    @pl.when(pl.program_id(1) == 0)
    def _():
        xb_ref[...] = x_ref[...].astype(BF16)
        m_ref[...] = _dot(xb_ref[...], wm_ref[...])

    o_ref[...] = _dot(xb_ref[...], w_ref[...]).astype(o_ref.dtype)


def _inproj(x2d, w_main, w_misc, tm=1024, tn=640):
    T, D = x2d.shape
    tm = min(tm, T)
    return pl.pallas_call(
        _inproj_kernel,
        grid=(T // tm, N_MAIN // tn),
        in_specs=[pl.BlockSpec((tm, D), lambda i, j: (i, 0)),
                  pl.BlockSpec((D, tn), lambda i, j: (0, j)),
                  pl.BlockSpec((D, LANES), lambda i, j: (0, 0))],
        out_specs=[pl.BlockSpec((tm, tn), lambda i, j: (i, j)),
                   pl.BlockSpec((tm, LANES), lambda i, j: (i, 0))],
        out_shape=[jax.ShapeDtypeStruct((T, N_MAIN), BF16),
                   jax.ShapeDtypeStruct((T, LANES), F32)],
        scratch_shapes=[pltpu.VMEM((tm, D), BF16)],
        compiler_params=_cparams(("parallel", "arbitrary")),
        name="inproj",
    )(x2d, w_main, w_misc)


def _gelu_tanh(x):
    return 0.5 * x * (1.0 + jnp.tanh(math.sqrt(2.0 / math.pi) * (x + 0.044715 * (x * x * x))))


def _cmp_kernel(x_ref, pe_ref, w1_ref, w2_ref, o_ref):
    nc = x_ref.shape[0]
    half = x_ref.shape[1]
    w1 = w1_ref[...].astype(BF16)
    pq = _dot(x_ref[...], w1[:half]), _dot(x_ref[...], w1[half:])
    pe_term = _dot(pe_ref[...].astype(BF16), w1)[0:1]
    h = pq[0] + pltpu.roll(pq[1], nc - 1, axis=0) + pe_term
    o_ref[...] = _dot(_gelu_tanh(h).astype(BF16), w2_ref[...].astype(BF16)).astype(o_ref.dtype)


def _nsa_compress(xr, pe, w1, w2):
    _, B, G, NC, W = xr.shape
    return pl.pallas_call(
        _cmp_kernel,
        grid=(2, B, G),
        in_specs=[pl.BlockSpec((None, None, None, NC, W), lambda s, b, g: (s, b, g, 0, 0)),
                  pl.BlockSpec((None, SUBLANES, 2 * W), lambda s, b, g: (s, 0, 0)),
                  pl.BlockSpec((None, 2 * W, NSA_CMP_HIDDEN), lambda s, b, g: (s, 0, 0)),
                  pl.BlockSpec((None, NSA_CMP_HIDDEN, HEAD_DIM), lambda s, b, g: (s, 0, 0))],
        out_specs=pl.BlockSpec((None, None, None, NC, HEAD_DIM), lambda s, b, g: (s, b, g, 0, 0)),
        out_shape=jax.ShapeDtypeStruct((2, B, G, NC, HEAD_DIM), BF16),
        compiler_params=_cparams(("parallel", "parallel", "parallel")),
        name="nsa_compress",
    )(xr, pe, w1, w2)


def _alibi_slopes():
    return [float(2.0 ** (-8.0 * (i + 1) / NSA_HEADS)) for i in range(NSA_HEADS)]


def _row_reduce(x, op, lane_reduce):
    n = x.shape[-1]
    acc = x[..., 0:LANES]
    for i in range(1, n // LANES):
        acc = op(acc, x[..., i * LANES:(i + 1) * LANES])
    return lane_reduce(acc, axis=-1, keepdims=True)


def _row_max(x):
    return _row_reduce(x, jnp.maximum, jnp.max)


def _row_sum(x):
    return _row_reduce(x, jnp.add, jnp.sum)


def _softmax_masked(lg):
    m = _row_max(lg)
    e = jnp.exp(lg - m)
    return e * (1.0 / _row_sum(e))


def _nsa_kernel(q_ref, misc_ref, kc_ref, vc_ref, ks_ref, vs_ref, kw_ref, vw_ref, ovt_ref, o_ref,
                m_sc, acc_sc, *, seq):
    H = NSA_HPG
    G = NSA_KV_HEADS
    R = H * QB
    nc = kc_ref.shape[0]
    nb = seq // NSA_SEL_LEN
    n_sel = min(NSA_N_SEL, nb)
    bpc = SEL_CHUNK // NSA_SEL_LEN
    q0 = pl.program_id(1) * QB
    t_q = q0 + lax.broadcasted_iota(jnp.int32, (QB, 1), 0)
    t3 = t_q[None]
    lane = lax.broadcasted_iota(jnp.int32, (QB, LANES), 1)
    gates = jax.nn.sigmoid(misc_ref[...])
    slopes = _alibi_slopes()

    def head_bias(g, pos):
        slope3 = jnp.concatenate(
            [jnp.full((1, 1, 1), slopes[g * H + h], F32) for h in range(H)], axis=0)
        return slope3 * pos.astype(F32)

    qps = []
    for g in range(G):
        in_g = (lane >= g * HEAD_DIM) & (lane < (g + 1) * HEAD_DIM)
        parts = []
        for h in range(H):
            hh = g * H + h
            slab = q_ref[:, (hh // 2) * LANES:(hh // 2 + 1) * LANES].astype(F32)
            if hh % 2 != g:
                slab = pltpu.roll(slab, HEAD_DIM, axis=1)
            parts.append(jnp.where(in_g, slab, 0.0).astype(BF16))
        qps.append(jnp.concatenate(parts, axis=0))

    cmp_end = lax.broadcasted_iota(jnp.int32, (1, 1, nc), 2) * NSA_CMP_STRIDE + (NSA_CMP_LEN - 1)
    neg_c = jnp.where(t3 >= cmp_end, 0.0, NEG_INF)
    row_ok = (t_q >= NSA_CMP_LEN - 1).astype(F32)
    t_row = q0 + lax.broadcasted_iota(jnp.int32, (1, QB), 1)
    blk_t = lax.broadcasted_iota(jnp.int32, (nb, QB), 0)
    cur_t = t_row // NSA_SEL_LEN
    forced = (blk_t == 0) | (blk_t == cur_t) | (blk_t == cur_t - 1)
    free = (blk_t <= cur_t) & jnp.logical_not(forced)
    o_cs, sels_t = [], []
    for g in range(G):
        lc = _dot_nt(qps[g], kc_ref[...]).reshape(H, QB, nc)
        p_c = _softmax_masked(lc + head_bias(g, cmp_end) + neg_c) * row_ok[None]
        o_cs.append(_dot(p_c.reshape(R, nc).astype(BF16), vc_ref[...]))
        p_sum = jnp.sum(p_c, axis=0)
        imp_t = sum(_dot_nt(ovt_ref[...], piece) for piece in _split3(p_sum))
        score = jnp.where(free, imp_t, -3e38)
        sel_t = forced.astype(F32)
        for _ in range(n_sel - 3):
            mx = jnp.max(score, axis=0, keepdims=True)
            first = jnp.min(jnp.where(score == mx, blk_t, nb), axis=0, keepdims=True)
            pick = blk_t == first
            sel_t = jnp.where(pick, 1.0, sel_t)
            score = jnp.where(pick, -3e38, score)
        sels_t.append(sel_t)
    sel_bf = [s.T.astype(BF16) for s in sels_t]

    m_sc[...] = jnp.full(m_sc.shape, NEG_INF, F32)
    acc_sc[...] = jnp.zeros(acc_sc.shape, F32)
    n_chunks = (q0 + QB + SEL_CHUNK - 1) // SEL_CHUNK
    lane_row = lax.broadcasted_iota(jnp.int32, (1, LANES), 1)
    own_lanes = [(lane_row >= g * HEAD_DIM) & (lane_row < (g + 1) * HEAD_DIM) for g in range(G)]

    def with_ones(v, g):
        return jnp.where(own_lanes[g], v, jnp.ones((), v.dtype))

    def exp_tiles(lg, m_rep):
        n = lg.shape[-1] // LANES
        return jnp.concatenate([jnp.exp(lg[..., i * LANES:(i + 1) * LANES] - m_rep) for i in range(n)], axis=-1)

    def sel_step(i, carry):
        c = n_chunks - 1 - i
        k0 = pl.multiple_of(c * SEL_CHUNK, SEL_CHUNK)
        in_chunk = (blk_t // bpc) == c
        s_pos = k0 + lax.broadcasted_iota(jnp.int32, (1, 1, SEL_CHUNK), 2)
        causal = t3 >= s_pos
        ej = lax.broadcasted_iota(jnp.int32, (nb, SEL_CHUNK), 0)
        es = lax.broadcasted_iota(jnp.int32, (nb, SEL_CHUNK), 1)
        expand = (ej == c * bpc + es // NSA_SEL_LEN).astype(BF16)
        for g in range(G):
            hit = jnp.max(jnp.where(in_chunk, sels_t[g], 0.0))

            @pl.when(hit > 0.0)
            def _():
                key_sel = _dot(sel_bf[g], expand)
                neg = jnp.where((key_sel[None] > 0.5) & causal, 0.0, NEG_INF)
                ls = _dot_nt(qps[g], ks_ref[pl.ds(k0, SEL_CHUNK), :]).reshape(H, QB, SEL_CHUNK)
                lg = ls + head_bias(g, s_pos) + neg
                m_old = m_sc[g].reshape(H, QB, LANES)
                m_new = jnp.maximum(m_old, _row_max(lg))
                alpha = jnp.exp(m_old - m_new)
                pr = exp_tiles(lg, m_new)
                pv = _dot(pr.reshape(R, SEL_CHUNK).astype(BF16), with_ones(vs_ref[pl.ds(k0, SEL_CHUNK), :], g))
                acc_sc[g] = alpha.reshape(R, LANES) * acc_sc[g] + pv
                m_sc[g] = m_new.reshape(R, LANES)

        return carry

    lax.fori_loop(0, n_chunks, sel_step, 0)

    wk = NSA_WINDOW + QB
    w0 = pl.multiple_of(jnp.maximum(q0 - NSA_WINDOW, 0), QB)
    w_pos = w0 + lax.broadcasted_iota(jnp.int32, (1, 1, wk), 2)
    dist_w = t3 - w_pos
    neg_w = jnp.where((dist_w >= 0) & (dist_w < NSA_WINDOW), 0.0, NEG_INF)

    for g in range(G):
        acc = acc_sc[g]
        o_s = acc / pltpu.roll(acc, HEAD_DIM, axis=1)
        lw = _dot_nt(qps[g], kw_ref[pl.ds(w0, wk), :]).reshape(H, QB, wk) + head_bias(g, w_pos) + neg_w
        e_w = jnp.exp(lw - _row_max(lw))
        ow = _dot(e_w.reshape(R, wk).astype(BF16), with_ones(vw_ref[pl.ds(w0, wk), :], g))
        o_w = ow / pltpu.roll(ow, HEAD_DIM, axis=1)
        o_c = o_cs[g]

        outs = []
        for h in range(H):
            hh = g * H + h
            rows = slice(h * QB, (h + 1) * QB)
            gc = gates[:, hh * 3 + 0:hh * 3 + 1]
            gs = gates[:, hh * 3 + 1:hh * 3 + 2]
            gw = gates[:, hh * 3 + 2:hh * 3 + 3]
            outs.append(gc * o_c[rows] + gs * o_s[rows] + gw * o_w[rows])
        for pair in range(H // 2):
            a, b = outs[2 * pair], outs[2 * pair + 1]
            if g == 0:
                slab = jnp.where(lane < HEAD_DIM, a, pltpu.roll(b, HEAD_DIM, axis=1))
            else:
                slab = jnp.where(lane < HEAD_DIM, pltpu.roll(a, HEAD_DIM, axis=1), b)
            col = (g * H // 2 + pair) * LANES
            o_ref[:, col:col + LANES] = slab.astype(o_ref.dtype)


def _overlap_matrix_t(nc, nb):
    k = np.arange(nc)[:, None]
    j = np.arange(nb)[None, :]
    ratio = NSA_SEL_LEN // NSA_CMP_STRIDE
    ov = (k < ratio * j + ratio) & (k + NSA_CMP_LEN // NSA_CMP_STRIDE > ratio * j) & (k < nc - 1)
    return jnp.asarray(ov.T.astype(np.float32), dtype=BF16)


def _nsa_attention(proj, misc, kcmp, vcmp, B, S):
    T = B * S
    nqb = S // QB
    NC = kcmp.shape[1]
    nb = S // NSA_SEL_LEN
    assert min(NSA_N_SEL, nb) > 3 and S >= NSA_WINDOW + QB
    R = NSA_HPG * QB
    G = NSA_KV_HEADS
    kv0 = COL_KV // LANES

    def slab(j):
        return pl.BlockSpec((S, LANES), lambda b, i, j=j: (b, kv0 + j))

    return pl.pallas_call(
        functools.partial(_nsa_kernel, seq=S),
        grid=(B, nqb),
        in_specs=[pl.BlockSpec((QB, D_NSA), lambda b, i: (b * nqb + i, 0)),
                  pl.BlockSpec((QB, LANES), lambda b, i: (b * nqb + i, 0)),
                  pl.BlockSpec((None, NC, LANES), lambda b, i: (b, 0, 0)),
                  pl.BlockSpec((None, NC, LANES), lambda b, i: (b, 0, 0)),
                  slab(2), slab(3), slab(4), slab(5),
                  pl.BlockSpec((nb, NC), lambda b, i: (0, 0))],
        out_specs=pl.BlockSpec((QB, D_NSA), lambda b, i: (b * nqb + i, 0)),
        out_shape=jax.ShapeDtypeStruct((T, D_NSA), F32),
        scratch_shapes=[pltpu.VMEM((G, R, LANES), F32), pltpu.VMEM((G, R, LANES), F32)],
        compiler_params=_cparams(("parallel", "arbitrary")),
        name="nsa_attention",
    )(proj, misc, kcmp, vcmp, proj, proj, proj, proj, _overlap_matrix_t(NC, nb))


def _softplus(z):
    return jnp.maximum(z, 0.0) + jnp.log(1.0 + jnp.exp(-jnp.abs(z)))


def _sb_kernel(q_ref, k_ref, v_ref, o_ref, c_sc, acc_sc):
    tq = q_ref.shape[0]
    qi = pl.program_id(2)
    lane = lax.broadcasted_iota(jnp.int32, (tq, LANES), 1)
    jr = lax.broadcasted_iota(jnp.int32, (tq, tq), 0)
    sc = lax.broadcasted_iota(jnp.int32, (tq, tq), 1)
    upper = (jr > sc).astype(BF16)
    diag_mask = sc < jr
    qf = q_ref[...].astype(F32)
    res = []
    for h in range(2):
        in_h = (lane >= h * HEAD_DIM) & (lane < (h + 1) * HEAD_DIM)
        qp = jnp.where(in_h, qf, 0.0).astype(BF16)
        c_sc[...] = jnp.zeros(c_sc.shape, F32)
        acc_sc[...] = jnp.zeros(acc_sc.shape, F32)

        def block(kb, mask):
            k0 = pl.multiple_of(kb * tq, tq)
            z = _dot_nt(qp, k_ref[pl.ds(k0, tq), :])
            sp = _softplus(z)
            log1m = -sp if mask is None else jnp.where(mask, -sp, 0.0)
            hi, mid, lo = _split3(log1m)
            tail = _dot(hi, upper) + _dot(mid, upper) + _dot(lo, upper)
            arg = (z - sp) + tail + c_sc[...]
            a = jnp.exp(arg)
            if mask is not None:
                a = jnp.where(mask, a, 0.0)
            acc_sc[...] += _dot(a.astype(BF16), v_ref[pl.ds(k0, tq), :])
            c_sc[...] += jnp.sum(log1m, axis=-1, keepdims=True)

        block(qi, diag_mask)

        def cond(st):
            kb, cmax = st
            return (kb >= 0) & (cmax > SB_SKIP_BELOW)

        def body(st):
            kb, _ = st
            block(kb, None)
            return kb - 1, jnp.max(c_sc[...])

        lax.while_loop(cond, body, (qi - 1, jnp.max(c_sc[...])))
        res.append(acc_sc[...])
    o_ref[...] = jnp.where(lane < HEAD_DIM, res[0], res[1]).astype(o_ref.dtype)


def _sb_attention(proj, B, S):
    T = B * S
    tq = min(SB_TQ, S)
    nq = S // tq
    nslab = D_SB // LANES
    return pl.pallas_call(
        _sb_kernel,
        grid=(B, nslab, nq),
        in_specs=[pl.BlockSpec((tq, LANES), lambda b, s, i: (b * nq + i, COL_SBQ // LANES + s)),
                  pl.BlockSpec((S, LANES), lambda b, s, i: (b, COL_SBK // LANES + s)),
                  pl.BlockSpec((S, LANES), lambda b, s, i: (b, COL_SBV // LANES + s))],
        out_specs=pl.BlockSpec((tq, LANES), lambda b, s, i: (b * nq + i, s)),
        out_shape=jax.ShapeDtypeStruct((T, D_SB), F32),
        scratch_shapes=[pltpu.VMEM((tq, 1), F32), pltpu.VMEM((tq, LANES), F32)],
        compiler_params=_cparams(("parallel", "parallel", "arbitrary")),
        name="sb_attention",
    )(proj, proj, proj)


def _ssd_kernel(z_ref, xbc_ref, misc_ref, cw_ref, cb_ref, dtb_ref, alog_ref, dskip_ref, g_ref,
                o_ref, xbuf, state):
    L = SSD_L
    c = pl.program_id(1)

    @pl.when(c == 0)
    def _():
        xbuf[0:SUBLANES, :] = jnp.zeros((SUBLANES, CONV_CH), F32)
        state[...] = jnp.zeros(state.shape, F32)

    xbuf[SUBLANES:SUBLANES + L, :] = xbc_ref[...].astype(F32)
    conv = cb_ref[...]
    for w in range(SSM_CONV):
        off = SUBLANES - (SSM_CONV - 1) + w
        conv = conv + xbuf[off:off + L, :] * cw_ref[w:w + 1, :]
    xbuf[0:SUBLANES, :] = xbuf[L:L + SUBLANES, :]
    xbc = conv * jax.nn.sigmoid(conv)
    xs = xbc[:, :D_SSM]
    b_in = [xbc[:, D_SSM + g * SSM_STATE:D_SSM + (g + 1) * SSM_STATE] for g in range(SSM_GROUPS)]
    c0 = D_SSM + SSM_GROUPS * SSM_STATE
    c_in = [xbc[:, c0 + g * SSM_STATE:c0 + (g + 1) * SSM_STATE].astype(BF16) for g in range(SSM_GROUPS)]

    dt = _softplus(misc_ref[...] + dtb_ref[...])
    a_neg = -jnp.exp(alog_ref[...])
    d_a = dt * a_neg
    rr = lax.broadcasted_iota(jnp.int32, (L, L), 0)
    cc = lax.broadcasted_iota(jnp.int32, (L, L), 1)
    causal = cc <= rr
    tril = causal.astype(BF16)
    a_cum = sum(_dot(tril, piece) for piece in _split3(d_a))
    a_cum_t = a_cum.T
    lane = lax.broadcasted_iota(jnp.int32, (L, LANES), 1)
    b_t = [b.T.astype(F32) for b in b_in]
    cb = [_dot_nt(c_in[g], b_in[g].astype(BF16)) for g in range(SSM_GROUPS)]

    ys = []
    for pair in range(SSM_HEADS // 2):
        h0, h1 = 2 * pair, 2 * pair + 1
        dt_pair = jnp.where(lane < SSM_HEAD_DIM, dt[:, MISC_DT + h0:MISC_DT + h0 + 1],
                            dt[:, MISC_DT + h1:MISC_DT + h1 + 1])
        xs_pair = xs[:, pair * LANES:(pair + 1) * LANES]
        xdt = (xs_pair * dt_pair).astype(BF16)
        y_pair = []
        for h in (h0, h1):
            g = h // (SSM_HEADS // SSM_GROUPS)
            col = a_cum[:, MISC_DT + h:MISC_DT + h + 1]
            row = a_cum_t[MISC_DT + h:MISC_DT + h + 1, :]
            last = row[:, L - 1:L]
            decay = jnp.exp(jnp.where(causal, col - row, NEG_INF))
            y = _dot((cb[g] * decay).astype(BF16), xdt)
            y = y + _dot(c_in[g], state[h].astype(BF16)) * jnp.exp(col)
            new_state = state[h] * jnp.exp(last) + _dot((b_t[g] * jnp.exp(last - row)).astype(BF16), xdt)
            state[h] = new_state
            y_pair.append(y)
        ys.append(jnp.where(lane < SSM_HEAD_DIM, y_pair[0], y_pair[1]))
    y = jnp.concatenate(ys, axis=1) + dskip_ref[...] * xs
    zf = z_ref[...].astype(F32)
    y = y * (zf * jax.nn.sigmoid(zf))
    y = y * lax.rsqrt(jnp.mean(y * y, axis=-1, keepdims=True) + 1e-6) * g_ref[...]
    o_ref[...] = y.astype(o_ref.dtype)


def _ssd_mixer(proj, misc, conv_w, conv_b, dtb_pad, alog_pad, dskip_full, norm_g, B, S):
    T = B * S
    nc = S // SSD_L

    def row(width):
        return pl.BlockSpec((1, width), lambda b, c: (0, 0))

    return pl.pallas_call(
        _ssd_kernel,
        grid=(B, nc),
        in_specs=[pl.BlockSpec((SSD_L, D_SSM), lambda b, c: (b * nc + c, COL_Z // D_SSM)),
                  pl.BlockSpec((SSD_L, CONV_CH), lambda b, c: (b * nc + c, COL_XBC // CONV_CH)),
                  pl.BlockSpec((SSD_L, LANES), lambda b, c: (b * nc + c, 0)),
                  pl.BlockSpec((SUBLANES, CONV_CH), lambda b, c: (0, 0)),
                  row(CONV_CH), row(LANES), row(LANES), row(D_SSM), row(D_SSM)],
        out_specs=pl.BlockSpec((SSD_L, D_SSM), lambda b, c: (b * nc + c, 0)),
        out_shape=jax.ShapeDtypeStruct((T, D_SSM), F32),
        scratch_shapes=[pltpu.VMEM((SSD_L + SUBLANES, CONV_CH), F32),
                        pltpu.VMEM((SSM_HEADS, SSM_STATE, LANES), F32)],
        compiler_params=_cparams(("parallel", "arbitrary")),
        name="ssd_mixer",
    )(proj, proj, misc, conv_w, conv_b, dtb_pad, alog_pad, dskip_full, norm_g)


def _layer_norm_rows(v, g, b):
    mu = jnp.mean(v, axis=-1, keepdims=True)
    d = v - mu
    var = jnp.mean(d * d, axis=-1, keepdims=True)
    return d * lax.rsqrt(var + 1e-5) * g + b


def _rms_rows(v, g):
    return v * lax.rsqrt(jnp.mean(v * v, axis=-1, keepdims=True) + 1e-6) * g


def _route(logits_t):
    tm = logits_t.shape[1]
    m = jnp.max(logits_t, axis=0, keepdims=True)
    e = jnp.exp(logits_t - m)
    probs = e / jnp.sum(e, axis=0, keepdims=True)
    p3 = probs.reshape(N_EXPERT_GROUPS, EXPERTS_PER_GROUP, tm)
    io_e = lax.broadcasted_iota(jnp.int32, p3.shape, 1)

    def top2(p, io, n):
        m1 = jnp.max(p, axis=-2, keepdims=True)
        i1 = jnp.min(jnp.where(p == m1, io, n), axis=-2, keepdims=True)
        rest = jnp.where(io == i1, -1.0, p)
        m2 = jnp.max(rest, axis=-2, keepdims=True)
        i2 = jnp.min(jnp.where(rest == m2, io, n), axis=-2, keepdims=True)
        return m1, i1, m2, i2

    m1, _, m2, _ = top2(p3, io_e, EXPERTS_PER_GROUP)
    gscore = (m1 + m2)[:, 0, :]
    io_g = lax.broadcasted_iota(jnp.int32, gscore.shape, 0)
    gmax = jnp.max(gscore, axis=0, keepdims=True)
    gsel = jnp.min(jnp.where(gscore == gmax, io_g, N_EXPERT_GROUPS), axis=0, keepdims=True)
    in_group = jnp.sum(jnp.where(io_g[:, None, :] == gsel[None], p3, 0.0), axis=0)
    io_l = lax.broadcasted_iota(jnp.int32, in_group.shape, 0)
    w1, e1, w2, e2 = top2(in_group, io_l, EXPERTS_PER_GROUP)
    wsum = w1 + w2
    ids = jnp.concatenate([gsel * EXPERTS_PER_GROUP + e1, gsel * EXPERTS_PER_GROUP + e2], axis=0)
    wts = jnp.concatenate([w1 / wsum, w2 / wsum], axis=0)
    return ids, wts


def _outproj_kernel(onsa_ref, osb_ref, ossm_ref, x_ref, w_ref, gn_ref, gs_ref, lg_ref, lb_ref,
                    rw_ref, rb_ref, h_ref, hb_ref, ids_ref, wts_ref, *, alpha):
    on = _rms_rows(onsa_ref[...], gn_ref[...]).astype(BF16)
    os_ = _rms_rows(osb_ref[...], gs_ref[...]).astype(BF16)
    om = ossm_ref[...].astype(BF16)
    mix = (_dot(on, w_ref[0:D_NSA, :]) + _dot(os_, w_ref[D_NSA:D_NSA + D_SB, :])
           + _dot(om, w_ref[D_NSA + D_SB:, :]))
    h = _layer_norm_rows(alpha * x_ref[...] + mix, lg_ref[...], lb_ref[...])
    h_ref[...] = h
    hb_ref[...] = h.astype(BF16)
    h_hi, h_mid, _ = _split3(h)
    r_hi, r_mid, _ = _split3(rw_ref[...])
    logits_t = (_dot_nt(r_hi, h_hi) + _dot_nt(r_hi, h_mid) + _dot_nt(r_mid, h_hi)) + rb_ref[...]
    ids, wts = _route(logits_t)
    pad_i = jnp.zeros((SUBLANES - TOP_K, ids.shape[1]), jnp.int32)
    ids_ref[...] = jnp.concatenate([ids, pad_i], axis=0)
    wts_ref[...] = jnp.concatenate([wts, pad_i.astype(F32)], axis=0)


def _outproj_ln_router(onsa, osb, ossm, x2d, w_out, gn, gs, lg, lb, rw_t, rb, alpha, layer):
    T, D = x2d.shape
    tm = min(MM_TM, T)

    def rows(width):
        return pl.BlockSpec((tm, width), lambda i: (i, 0))

    def const(shape):
        return pl.BlockSpec(shape, lambda i: (0,) * len(shape))

    return pl.pallas_call(
        functools.partial(_outproj_kernel, alpha=alpha),
        grid=(T // tm,),
        in_specs=[rows(D_NSA), rows(D_SB), rows(D_SSM), rows(D),
                  pl.BlockSpec((None, D, D), lambda i: (layer, 0, 0)),
                  const((1, D_NSA)), const((1, D_SB)), const((1, D)), const((1, D)),
                  const((N_EXPERTS, D)), const((N_EXPERTS, 1))],
        out_specs=[rows(D), rows(D),
                   pl.BlockSpec((SUBLANES, tm), lambda i: (0, i)),
                   pl.BlockSpec((SUBLANES, tm), lambda i: (0, i))],
        out_shape=[jax.ShapeDtypeStruct((T, D), F32), jax.ShapeDtypeStruct((T, D), BF16),
                   jax.ShapeDtypeStruct((SUBLANES, T), jnp.int32),
                   jax.ShapeDtypeStruct((SUBLANES, T), F32)],
        compiler_params=_cparams(("parallel",)),
        name="outproj_ln_router",
    )(onsa, osb, ossm, x2d, w_out, gn, gs, lg, lb, rw_t, rb)


def _expert_kernel(be_ref, nu_ref, x_ref, wg_ref, wu_ref, wd_ref, o_ref):
    @pl.when(pl.program_id(0) < nu_ref[0])
    def _():
        x = x_ref[...]
        gate = _dot(x, wg_ref[...])
        up = _dot(x, wu_ref[...])
        hid = (gate * jax.nn.sigmoid(gate) * up).astype(BF16)
        o_ref[...] = _dot(hid, wd_ref[...]).astype(o_ref.dtype)

    @pl.when(pl.program_id(0) >= nu_ref[0])
    def _():
        o_ref[...] = jnp.zeros(o_ref.shape, o_ref.dtype)


def _expert_ffn(block_e, n_used, xs, wg, wu, wd, layer):
    n_rows, D = xs.shape
    n_blocks = n_rows // EXPERT_BLOCK
    d_ff = wg.shape[-1]
    grid_spec = pltpu.PrefetchScalarGridSpec(
        num_scalar_prefetch=2,
        grid=(n_blocks,),
        in_specs=[pl.BlockSpec((EXPERT_BLOCK, D), lambda i, be, nu: (i, 0)),
                  pl.BlockSpec((None, None, D, d_ff), lambda i, be, nu: (layer, be[i], 0, 0)),
                  pl.BlockSpec((None, None, D, d_ff), lambda i, be, nu: (layer, be[i], 0, 0)),
                  pl.BlockSpec((None, None, d_ff, D), lambda i, be, nu: (layer, be[i], 0, 0))],
        out_specs=pl.BlockSpec((EXPERT_BLOCK, D), lambda i, be, nu: (i, 0)),
    )
    return pl.pallas_call(
        _expert_kernel,
        grid_spec=grid_spec,
        out_shape=jax.ShapeDtypeStruct((n_rows, D), F32),
        compiler_params=_cparams(("arbitrary",)),
        name="expert_ffn",
    )(block_e, n_used, xs, wg, wu, wd)


def _ple_kernel(hb_ref, h_ref, p_ref, ffn_ref, wg_ref, wp_ref, lg_ref, lb_ref, o_ref, *, alpha):
    gate = jax.nn.sigmoid(_dot(hb_ref[...], wg_ref[...]))
    emb = _dot(p_ref[...].astype(BF16), wp_ref[...])
    v = alpha * h_ref[...] + ffn_ref[...] + gate * emb
    o_ref[...] = _layer_norm_rows(v, lg_ref[...], lb_ref[...])


def _ple_ln(hb, h, p3d, ffn, wg, wp, lg, lb, alpha, layer):
    T, D = h.shape
    tm = min(MM_TM, T)
    P = p3d.shape[-1]

    def rows(width):
        return pl.BlockSpec((tm, width), lambda i: (i, 0))

    def const(shape):
        return pl.BlockSpec(shape, lambda i: (0,) * len(shape))

    return pl.pallas_call(
        functools.partial(_ple_kernel, alpha=alpha),
        grid=(T // tm,),
        in_specs=[rows(D), rows(D), pl.BlockSpec((None, tm, P), lambda i: (layer, i, 0)), rows(D),
                  pl.BlockSpec((None, D, D), lambda i: (layer, 0, 0)),
                  pl.BlockSpec((None, P, D), lambda i: (layer, 0, 0)),
                  const((1, D)), const((1, D))],
        out_specs=rows(D),
        out_shape=jax.ShapeDtypeStruct((T, D), F32),
        compiler_params=_cparams(("parallel",)),
        name="ple_ln",
    )(hb, h, p3d, ffn, wg, wp, lg, lb)


def _dispatch_plan(ids, T):
    TK = T * TOP_K
    flat_e = ids.T.reshape(TK)
    order = jnp.argsort(flat_e)
    experts = jnp.arange(N_EXPERTS, dtype=jnp.int32)
    counts = jnp.sum((flat_e[:, None] == experts[None, :]).astype(jnp.int32), axis=0)
    padded = (counts + EXPERT_BLOCK - 1) // EXPERT_BLOCK * EXPERT_BLOCK
    start = jnp.cumsum(counts) - counts
    ends = jnp.cumsum(padded)
    pstart = ends - padded
    n_blocks = -(-TK // EXPERT_BLOCK) + N_EXPERTS
    n_rows = n_blocks * EXPERT_BLOCK
    blk0 = jnp.arange(n_blocks, dtype=jnp.int32) * EXPERT_BLOCK
    block_e = jnp.minimum(jnp.sum((ends[None, :] <= blk0[:, None]).astype(jnp.int32), axis=1), N_EXPERTS - 1)
    off = (blk0 - pstart[block_e])[:, None] + jnp.arange(EXPERT_BLOCK, dtype=jnp.int32)[None, :]
    valid = off < counts[block_e][:, None]
    src = jnp.clip(start[block_e][:, None] + off, 0, TK - 1)
    row_tok = jnp.where(valid, order[src] // TOP_K, 0).reshape(n_rows).astype(jnp.int32)
    se = flat_e[order]
    dest = pstart[se] + jnp.arange(TK, dtype=jnp.int32) - start[se]
    pos = dest[jnp.argsort(order)].reshape(T, TOP_K)
    n_used = (ends[-1] // EXPERT_BLOCK).astype(jnp.int32).reshape(1)
    return row_tok, pos, block_e.astype(jnp.int32), n_used


def _prep_w_in(w_in_l):
    d_kv = 6 * NSA_KV_HEADS * HEAD_DIM
    offs = np.cumsum([0, D_NSA, d_kv, NSA_HEADS * 3, 3 * D_SB, D_SSM, CONV_CH, SSM_HEADS])
    q, kv, gate, sbqkv, z, xbc, dt = [w_in_l[:, offs[i]:offs[i + 1]] for i in range(7)]
    scale = HEAD_DIM ** -0.5
    main = jnp.concatenate([q * scale, kv, sbqkv[:, :D_SB] * scale, sbqkv[:, D_SB:], z, xbc], axis=1)
    pad = jnp.zeros((w_in_l.shape[0], LANES - NSA_HEADS * 3 - SSM_HEADS), w_in_l.dtype)
    misc = jnp.concatenate([gate, dt, pad], axis=1)
    return main.astype(BF16), misc.astype(BF16)


def _pad_lanes(v, offset):
    out = jnp.zeros((1, LANES), F32)
    return lax.dynamic_update_slice(out, v.reshape(1, -1).astype(F32), (0, offset))


def kernel(x, p, w_in, w_out, cmp_pe_k, cmp_w1_k, cmp_w2_k, cmp_pe_v, cmp_w1_v, cmp_w2_v, nsa_norm_g, sb_norm_g, conv_w, conv_b, dt_bias, a_log, d_skip, ssm_norm_g, ln1_g, ln1_b, ln2_g, ln2_b, router_w, router_b, expert_w_gate, expert_w_up, expert_w_down, ple_gate_w, ple_proj_w):
    B, S, D = x.shape
    depth = w_in.shape[0]
    T = B * S
    alpha = (2 * depth) ** 0.25
    NC = S // NSA_CMP_STRIDE
    x2d = x.reshape(T, D)
    rw_t = router_w.T
    rb = router_b.reshape(N_EXPERTS, 1)
    w_out_b = w_out.astype(BF16)
    ew_gate_b = expert_w_gate.astype(BF16)
    ew_up_b = expert_w_up.astype(BF16)
    ew_down_b = expert_w_down.astype(BF16)
    ple_gate_b = ple_gate_w.astype(BF16)
    ple_proj_b = ple_proj_w.astype(BF16)
    p3d = p.reshape(depth, T, -1)

    for i in range(depth):
        w_main, w_misc = _prep_w_in(w_in[i])
        proj, misc = _inproj(x2d, w_main, w_misc)

        raw = proj[:, COL_KV:COL_KV + 2 * LANES].reshape(B, S, 2, NSA_KV_HEADS, HEAD_DIM)
        xr = raw.transpose(2, 0, 3, 1, 4).reshape(2, B, NSA_KV_HEADS, NC, NSA_CMP_STRIDE * HEAD_DIM)
        pe = jnp.stack([cmp_pe_k[i], cmp_pe_v[i]]).reshape(2, 1, NSA_CMP_LEN * HEAD_DIM)
        pe = jnp.broadcast_to(pe, (2, SUBLANES, NSA_CMP_LEN * HEAD_DIM))
        cmp = _nsa_compress(xr, pe, jnp.stack([cmp_w1_k[i], cmp_w1_v[i]]),
                            jnp.stack([cmp_w2_k[i], cmp_w2_v[i]]))
        cmp = cmp.transpose(0, 1, 3, 2, 4).reshape(2, B, NC, LANES)
        o_nsa = _nsa_attention(proj, misc, cmp[0], cmp[1], B, S)

        o_sb = _sb_attention(proj, B, S)

        o_ssm = _ssd_mixer(
            proj, misc,
            jnp.concatenate([conv_w[i].reshape(SSM_CONV, CONV_CH),
                             jnp.zeros((SUBLANES - SSM_CONV, CONV_CH), F32)], axis=0),
            conv_b[i].reshape(1, CONV_CH), _pad_lanes(dt_bias[i], MISC_DT), _pad_lanes(a_log[i], MISC_DT),
            jnp.repeat(d_skip[i], SSM_HEAD_DIM).reshape(1, D_SSM), ssm_norm_g[i].reshape(1, D_SSM), B, S)

        h, hb, ids, wts = _outproj_ln_router(
            o_nsa, o_sb, o_ssm, x2d, w_out_b, nsa_norm_g[i].reshape(1, D_NSA),
            sb_norm_g[i].reshape(1, D_SB), ln1_g[i].reshape(1, D), ln1_b[i].reshape(1, D), rw_t, rb, alpha, i)

        row_tok, pos, block_e, n_used = _dispatch_plan(ids[:TOP_K], T)
        xs = hb.at[row_tok].get(mode="promise_in_bounds")
        ys = _expert_ffn(block_e, n_used, xs, ew_gate_b, ew_up_b, ew_down_b, i)
        w_tok = wts[:TOP_K].T
        ffn = (ys.at[pos[:, 0]].get(mode="promise_in_bounds") * w_tok[:, 0:1]
               + ys.at[pos[:, 1]].get(mode="promise_in_bounds") * w_tok[:, 1:2])

        x2d = _ple_ln(hb, h, p3d, ffn, ple_gate_b, ple_proj_b, ln2_g[i].reshape(1, D), ln2_b[i].reshape(1, D),
                      alpha, i)

    return x2d.reshape(B, S, D)
```

```python
import functools
import math

import jax
import jax.numpy as jnp
import numpy as np
from jax import lax
from jax.experimental import pallas as pl
from jax.experimental.pallas import tpu as pltpu

F32 = jnp.float32
BF16 = jnp.bfloat16

HEAD_DIM = 64
NSA_HEADS = 12
NSA_KV_HEADS = 2
NSA_HPG = NSA_HEADS // NSA_KV_HEADS
NSA_CMP_LEN = 32
NSA_CMP_STRIDE = 16
NSA_CMP_HIDDEN = 256
NSA_SEL_LEN = 64
NSA_N_SEL = 16
NSA_WINDOW = 512
FORCE_BONUS = 1e4
MASKED_SCORE = -1e9
NEG_INF = -1e30
SB_HEADS = 8
SSM_HEADS = 12
SSM_HEAD_DIM = 64
SSM_GROUPS = 2
SSM_STATE = 128
SSM_CONV = 4
D_NSA = NSA_HEADS * HEAD_DIM
D_SB = SB_HEADS * HEAD_DIM
D_SSM = SSM_HEADS * SSM_HEAD_DIM
CONV_CH = D_SSM + 2 * SSM_GROUPS * SSM_STATE
N_EXPERTS = 32
N_EXPERT_GROUPS = 4
EXPERTS_PER_GROUP = N_EXPERTS // N_EXPERT_GROUPS
TOP_K = 2

LANES = 128
SUBLANES = 8
VMEM_LIMIT_BYTES = 56 * 1024 * 1024

COL_Q = 0
COL_KV = COL_Q + D_NSA
COL_SBQ = COL_KV + 6 * LANES
COL_SBK = COL_SBQ + D_SB
COL_SBV = COL_SBK + D_SB
COL_Z = COL_SBV + D_SB
COL_XBC = COL_Z + D_SSM
N_MAIN = COL_XBC + CONV_CH
MISC_GATE = 0
MISC_DT = NSA_HEADS * 3

QB = 128
SEL_CHUNK = 512
SB_TQ = 256
SSD_L = 128
MM_TM = 256
EXPERT_BLOCK = 256
SB_SKIP_BELOW = -150.0


def _cparams(sem):
    return pltpu.CompilerParams(dimension_semantics=sem, vmem_limit_bytes=VMEM_LIMIT_BYTES)


def _split3(x):
    hi = x.astype(BF16)
    r1 = x - hi.astype(F32)
    mid = r1.astype(BF16)
    lo = (r1 - mid.astype(F32)).astype(BF16)
    return hi, mid, lo


def _dot(a, b):
    return jnp.dot(a, b, preferred_element_type=F32)


def _dot_nt(a, b):
    return lax.dot_general(a, b, (((1,), (1,)), ((), ())), preferred_element_type=F32)


def _inproj_kernel(x_ref, w_ref, wm_ref, o_ref, m_ref, xb_ref):
    @pl.when(pl.program_id(1) == 0)
    def _():
        xb_ref[...] = x_ref[...].astype(BF16)
        m_ref[...] = _dot(xb_ref[...], wm_ref[...])

    o_ref[...] = _dot(xb_ref[...], w_ref[...]).astype(o_ref.dtype)


def _inproj(x2d, w_main, w_misc, tm=1024, tn=640):
    T, D = x2d.shape
    tm = min(tm, T)
    return pl.pallas_call(
        _inproj_kernel,
        grid=(T // tm, N_MAIN // tn),
        in_specs=[pl.BlockSpec((tm, D), lambda i, j: (i, 0)),
                  pl.BlockSpec((D, tn), lambda i, j: (0, j)),
                  pl.BlockSpec((D, LANES), lambda i, j: (0, 0))],
        out_specs=[pl.BlockSpec((tm, tn), lambda i, j: (i, j)),
                   pl.BlockSpec((tm, LANES), lambda i, j: (i, 0))],
        out_shape=[jax.ShapeDtypeStruct((T, N_MAIN), BF16),
                   jax.ShapeDtypeStruct((T, LANES), F32)],
        scratch_shapes=[pltpu.VMEM((tm, D), BF16)],
        compiler_params=_cparams(("parallel", "arbitrary")),
        name="inproj",
    )(x2d, w_main, w_misc)


def _gelu_tanh(x):
    return 0.5 * x * (1.0 + jnp.tanh(math.sqrt(2.0 / math.pi) * (x + 0.044715 * (x * x * x))))


def _cmp_kernel(x_ref, pe_ref, w1_ref, w2_ref, o_ref):
    nc = x_ref.shape[0]
    half = x_ref.shape[1]
    w1 = w1_ref[...].astype(BF16)
    pq = _dot(x_ref[...], w1[:half]), _dot(x_ref[...], w1[half:])
    pe_term = _dot(pe_ref[...].astype(BF16), w1)[0:1]
    h = pq[0] + pltpu.roll(pq[1], nc - 1, axis=0) + pe_term
    o_ref[...] = _dot(_gelu_tanh(h).astype(BF16), w2_ref[...].astype(BF16)).astype(o_ref.dtype)


def _nsa_compress(xr, pe, w1, w2):
    _, B, G, NC, W = xr.shape
    return pl.pallas_call(
        _cmp_kernel,
        grid=(2, B, G),
        in_specs=[pl.BlockSpec((None, None, None, NC, W), lambda s, b, g: (s, b, g, 0, 0)),
                  pl.BlockSpec((None, SUBLANES, 2 * W), lambda s, b, g: (s, 0, 0)),
                  pl.BlockSpec((None, 2 * W, NSA_CMP_HIDDEN), lambda s, b, g: (s, 0, 0)),
                  pl.BlockSpec((None, NSA_CMP_HIDDEN, HEAD_DIM), lambda s, b, g: (s, 0, 0))],
        out_specs=pl.BlockSpec((None, None, None, NC, HEAD_DIM), lambda s, b, g: (s, b, g, 0, 0)),
        out_shape=jax.ShapeDtypeStruct((2, B, G, NC, HEAD_DIM), BF16),
        compiler_params=_cparams(("parallel", "parallel", "parallel")),
        name="nsa_compress",
    )(xr, pe, w1, w2)


def _alibi_slopes():
    return [float(2.0 ** (-8.0 * (i + 1) / NSA_HEADS)) for i in range(NSA_HEADS)]


def _row_reduce(x, op, lane_reduce):
    n = x.shape[-1]
    acc = x[..., 0:LANES]
    for i in range(1, n // LANES):
        acc = op(acc, x[..., i * LANES:(i + 1) * LANES])
    return lane_reduce(acc, axis=-1, keepdims=True)


def _row_max(x):
    return _row_reduce(x, jnp.maximum, jnp.max)


def _row_sum(x):
    return _row_reduce(x, jnp.add, jnp.sum)


def _softmax_masked(lg):
    m = _row_max(lg)
    e = jnp.exp(lg - m)
    return e * (1.0 / _row_sum(e))


def _nsa_kernel(q_ref, misc_ref, kc_ref, vc_ref, ks_ref, vs_ref, kw_ref, vw_ref, ovt_ref, o_ref,
                m_sc, acc_sc, *, seq):
    H = NSA_HPG
    G = NSA_KV_HEADS
    R = H * QB
    nc = kc_ref.shape[0]
    nb = seq // NSA_SEL_LEN
    n_sel = min(NSA_N_SEL, nb)
    bpc = SEL_CHUNK // NSA_SEL_LEN
    q0 = pl.program_id(1) * QB
    t_q = q0 + lax.broadcasted_iota(jnp.int32, (QB, 1), 0)
    t3 = t_q[None]
    lane = lax.broadcasted_iota(jnp.int32, (QB, LANES), 1)
    gates = jax.nn.sigmoid(misc_ref[...])
    slopes = _alibi_slopes()

    def head_bias(g, pos):
        slope3 = jnp.concatenate(
            [jnp.full((1, 1, 1), slopes[g * H + h], F32) for h in range(H)], axis=0)
        return slope3 * pos.astype(F32)

    qps = []
    for g in range(G):
        in_g = (lane >= g * HEAD_DIM) & (lane < (g + 1) * HEAD_DIM)
        parts = []
        for h in range(H):
            hh = g * H + h
            slab = q_ref[:, (hh // 2) * LANES:(hh // 2 + 1) * LANES].astype(F32)
            if hh % 2 != g:
                slab = pltpu.roll(slab, HEAD_DIM, axis=1)
            parts.append(jnp.where(in_g, slab, 0.0).astype(BF16))
        qps.append(jnp.concatenate(parts, axis=0))

    cmp_end = lax.broadcasted_iota(jnp.int32, (1, 1, nc), 2) * NSA_CMP_STRIDE + (NSA_CMP_LEN - 1)
    neg_c = jnp.where(t3 >= cmp_end, 0.0, NEG_INF)
    row_ok = (t_q >= NSA_CMP_LEN - 1).astype(F32)
    t_row = q0 + lax.broadcasted_iota(jnp.int32, (1, QB), 1)
    blk_t = lax.broadcasted_iota(jnp.int32, (nb, QB), 0)
    cur_t = t_row // NSA_SEL_LEN
    forced = (blk_t == 0) | (blk_t == cur_t) | (blk_t == cur_t - 1)
    free = (blk_t <= cur_t) & jnp.logical_not(forced)
    o_cs, sels_t = [], []
    for g in range(G):
        lc = _dot_nt(qps[g], kc_ref[...]).reshape(H, QB, nc)
        p_c = _softmax_masked(lc + head_bias(g, cmp_end) + neg_c) * row_ok[None]
        o_cs.append(_dot(p_c.reshape(R, nc).astype(BF16), vc_ref[...]))
        p_sum = jnp.sum(p_c, axis=0)
        imp_t = sum(_dot_nt(ovt_ref[...], piece) for piece in _split3(p_sum))
        score = jnp.where(free, imp_t, -3e38)
        sel_t = forced.astype(F32)
        for _ in range(n_sel - 3):
            mx = jnp.max(score, axis=0, keepdims=True)
            first = jnp.min(jnp.where(score == mx, blk_t, nb), axis=0, keepdims=True)
            pick = blk_t == first
            sel_t = jnp.where(pick, 1.0, sel_t)
            score = jnp.where(pick, -3e38, score)
        sels_t.append(sel_t)
    sel_bf = [s.T.astype(BF16) for s in sels_t]

    m_sc[...] = jnp.full(m_sc.shape, NEG_INF, F32)
    acc_sc[...] = jnp.zeros(acc_sc.shape, F32)
    n_chunks = (q0 + QB + SEL_CHUNK - 1) // SEL_CHUNK
    lane_row = lax.broadcasted_iota(jnp.int32, (1, LANES), 1)
    own_lanes = [(lane_row >= g * HEAD_DIM) & (lane_row < (g + 1) * HEAD_DIM) for g in range(G)]

    def with_ones(v, g):
        return jnp.where(own_lanes[g], v, jnp.ones((), v.dtype))

    def exp_tiles(lg, m_rep):
        n = lg.shape[-1] // LANES
        return jnp.concatenate([jnp.exp(lg[..., i * LANES:(i + 1) * LANES] - m_rep) for i in range(n)], axis=-1)

    def sel_step(i, carry):
        c = n_chunks - 1 - i
        k0 = pl.multiple_of(c * SEL_CHUNK, SEL_CHUNK)
        in_chunk = (blk_t // bpc) == c
        s_pos = k0 + lax.broadcasted_iota(jnp.int32, (1, 1, SEL_CHUNK), 2)
        causal = t3 >= s_pos
        ej = lax.broadcasted_iota(jnp.int32, (nb, SEL_CHUNK), 0)
        es = lax.broadcasted_iota(jnp.int32, (nb, SEL_CHUNK), 1)
        expand = (ej == c * bpc + es // NSA_SEL_LEN).astype(BF16)
        for g in range(G):
            hit = jnp.max(jnp.where(in_chunk, sels_t[g], 0.0))

            @pl.when(hit > 0.0)
            def _():
                key_sel = _dot(sel_bf[g], expand)
                neg = jnp.where((key_sel[None] > 0.5) & causal, 0.0, NEG_INF)
                ls = _dot_nt(qps[g], ks_ref[pl.ds(k0, SEL_CHUNK), :]).reshape(H, QB, SEL_CHUNK)
                lg = ls + head_bias(g, s_pos) + neg
                m_old = m_sc[g].reshape(H, QB, LANES)
                m_new = jnp.maximum(m_old, _row_max(lg))
                alpha = jnp.exp(m_old - m_new)
                pr = exp_tiles(lg, m_new)
                pv = _dot(pr.reshape(R, SEL_CHUNK).astype(BF16), with_ones(vs_ref[pl.ds(k0, SEL_CHUNK), :], g))
                acc_sc[g] = alpha.reshape(R, LANES) * acc_sc[g] + pv
                m_sc[g] = m_new.reshape(R, LANES)

        return carry

    lax.fori_loop(0, n_chunks, sel_step, 0)

    wk = NSA_WINDOW + QB
    w0 = pl.multiple_of(jnp.maximum(q0 - NSA_WINDOW, 0), QB)
    w_pos = w0 + lax.broadcasted_iota(jnp.int32, (1, 1, wk), 2)
    dist_w = t3 - w_pos
    neg_w = jnp.where((dist_w >= 0) & (dist_w < NSA_WINDOW), 0.0, NEG_INF)

    for g in range(G):
        acc = acc_sc[g]
        o_s = acc / pltpu.roll(acc, HEAD_DIM, axis=1)
        lw = _dot_nt(qps[g], kw_ref[pl.ds(w0, wk), :]).reshape(H, QB, wk) + head_bias(g, w_pos) + neg_w
        e_w = jnp.exp(lw - _row_max(lw))
        ow = _dot(e_w.reshape(R, wk).astype(BF16), with_ones(vw_ref[pl.ds(w0, wk), :], g))
        o_w = ow / pltpu.roll(ow, HEAD_DIM, axis=1)
        o_c = o_cs[g]

        outs = []
        for h in range(H):
            hh = g * H + h
            rows = slice(h * QB, (h + 1) * QB)
            gc = gates[:, hh * 3 + 0:hh * 3 + 1]
            gs = gates[:, hh * 3 + 1:hh * 3 + 2]
            gw = gates[:, hh * 3 + 2:hh * 3 + 3]
            outs.append(gc * o_c[rows] + gs * o_s[rows] + gw * o_w[rows])
        for pair in range(H // 2):
            a, b = outs[2 * pair], outs[2 * pair + 1]
            if g == 0:
                slab = jnp.where(lane < HEAD_DIM, a, pltpu.roll(b, HEAD_DIM, axis=1))
            else:
                slab = jnp.where(lane < HEAD_DIM, pltpu.roll(a, HEAD_DIM, axis=1), b)
            col = (g * H // 2 + pair) * LANES
            o_ref[:, col:col + LANES] = slab.astype(o_ref.dtype)


def _overlap_matrix_t(nc, nb):
    k = np.arange(nc)[:, None]
    j = np.arange(nb)[None, :]
    ratio = NSA_SEL_LEN // NSA_CMP_STRIDE
    ov = (k < ratio * j + ratio) & (k + NSA_CMP_LEN // NSA_CMP_STRIDE > ratio * j) & (k < nc - 1)
    return jnp.asarray(ov.T.astype(np.float32), dtype=BF16)


def _nsa_attention(proj, misc, kcmp, vcmp, B, S):
    T = B * S
    nqb = S // QB
    NC = kcmp.shape[1]
    nb = S // NSA_SEL_LEN
    assert min(NSA_N_SEL, nb) > 3 and S >= NSA_WINDOW + QB
    R = NSA_HPG * QB
    G = NSA_KV_HEADS
    kv0 = COL_KV // LANES

    def slab(j):
        return pl.BlockSpec((S, LANES), lambda b, i, j=j: (b, kv0 + j))

    return pl.pallas_call(
        functools.partial(_nsa_kernel, seq=S),
        grid=(B, nqb),
        in_specs=[pl.BlockSpec((QB, D_NSA), lambda b, i: (b * nqb + i, 0)),
                  pl.BlockSpec((QB, LANES), lambda b, i: (b * nqb + i, 0)),
                  pl.BlockSpec((None, NC, LANES), lambda b, i: (b, 0, 0)),
                  pl.BlockSpec((None, NC, LANES), lambda b, i: (b, 0, 0)),
                  slab(2), slab(3), slab(4), slab(5),
                  pl.BlockSpec((nb, NC), lambda b, i: (0, 0))],
        out_specs=pl.BlockSpec((QB, D_NSA), lambda b, i: (b * nqb + i, 0)),
        out_shape=jax.ShapeDtypeStruct((T, D_NSA), F32),
        scratch_shapes=[pltpu.VMEM((G, R, LANES), F32), pltpu.VMEM((G, R, LANES), F32)],
        compiler_params=_cparams(("parallel", "arbitrary")),
        name="nsa_attention",
    )(proj, misc, kcmp, vcmp, proj, proj, proj, proj, _overlap_matrix_t(NC, nb))


def _softplus(z):
    return jnp.maximum(z, 0.0) + jnp.log(1.0 + jnp.exp(-jnp.abs(z)))


def _sb_kernel(q_ref, k_ref, v_ref, o_ref, c_sc, acc_sc):
    tq = q_ref.shape[0]
    qi = pl.program_id(2)
    lane = lax.broadcasted_iota(jnp.int32, (tq, LANES), 1)
    jr = lax.broadcasted_iota(jnp.int32, (tq, tq), 0)
    sc = lax.broadcasted_iota(jnp.int32, (tq, tq), 1)
    upper = (jr > sc).astype(BF16)
    diag_mask = sc < jr
    qf = q_ref[...].astype(F32)
    res = []
    for h in range(2):
        in_h = (lane >= h * HEAD_DIM) & (lane < (h + 1) * HEAD_DIM)
        qp = jnp.where(in_h, qf, 0.0).astype(BF16)
        c_sc[...] = jnp.zeros(c_sc.shape, F32)
        acc_sc[...] = jnp.zeros(acc_sc.shape, F32)

        def block(kb, mask):
            k0 = pl.multiple_of(kb * tq, tq)
            z = _dot_nt(qp, k_ref[pl.ds(k0, tq), :])
            sp = _softplus(z)
            log1m = -sp if mask is None else jnp.where(mask, -sp, 0.0)
            hi, mid, lo = _split3(log1m)
            tail = _dot(hi, upper) + _dot(mid, upper) + _dot(lo, upper)
            arg = (z - sp) + tail + c_sc[...]
            a = jnp.exp(arg)
            if mask is not None:
                a = jnp.where(mask, a, 0.0)
            acc_sc[...] += _dot(a.astype(BF16), v_ref[pl.ds(k0, tq), :])
            c_sc[...] += jnp.sum(log1m, axis=-1, keepdims=True)

        block(qi, diag_mask)

        def cond(st):
            kb, cmax = st
            return (kb >= 0) & (cmax > SB_SKIP_BELOW)

        def body(st):
            kb, _ = st
            block(kb, None)
            return kb - 1, jnp.max(c_sc[...])

        lax.while_loop(cond, body, (qi - 1, jnp.max(c_sc[...])))
        res.append(acc_sc[...])
    o_ref[...] = jnp.where(lane < HEAD_DIM, res[0], res[1]).astype(o_ref.dtype)


def _sb_attention(proj, B, S):
    T = B * S
    tq = min(SB_TQ, S)
    nq = S // tq
    nslab = D_SB // LANES
    return pl.pallas_call(
        _sb_kernel,
        grid=(B, nslab, nq),
        in_specs=[pl.BlockSpec((tq, LANES), lambda b, s, i: (b * nq + i, COL_SBQ // LANES + s)),
                  pl.BlockSpec((S, LANES), lambda b, s, i: (b, COL_SBK // LANES + s)),
                  pl.BlockSpec((S, LANES), lambda b, s, i: (b, COL_SBV // LANES + s))],
        out_specs=pl.BlockSpec((tq, LANES), lambda b, s, i: (b * nq + i, s)),
        out_shape=jax.ShapeDtypeStruct((T, D_SB), F32),
        scratch_shapes=[pltpu.VMEM((tq, 1), F32), pltpu.VMEM((tq, LANES), F32)],
        compiler_params=_cparams(("parallel", "parallel", "arbitrary")),
        name="sb_attention",
    )(proj, proj, proj)


def _ssd_kernel(z_ref, xbc_ref, misc_ref, cw_ref, cb_ref, dtb_ref, alog_ref, dskip_ref, g_ref,
                o_ref, xbuf, state):
    L = SSD_L
    c = pl.program_id(1)

    @pl.when(c == 0)
    def _():
        xbuf[0:SUBLANES, :] = jnp.zeros((SUBLANES, CONV_CH), F32)
        state[...] = jnp.zeros(state.shape, F32)

    xbuf[SUBLANES:SUBLANES + L, :] = xbc_ref[...].astype(F32)
    conv = cb_ref[...]
    for w in range(SSM_CONV):
        off = SUBLANES - (SSM_CONV - 1) + w
        conv = conv + xbuf[off:off + L, :] * cw_ref[w:w + 1, :]
    xbuf[0:SUBLANES, :] = xbuf[L:L + SUBLANES, :]
    xbc = conv * jax.nn.sigmoid(conv)
    xs = xbc[:, :D_SSM]
    b_in = [xbc[:, D_SSM + g * SSM_STATE:D_SSM + (g + 1) * SSM_STATE] for g in range(SSM_GROUPS)]
    c0 = D_SSM + SSM_GROUPS * SSM_STATE
    c_in = [xbc[:, c0 + g * SSM_STATE:c0 + (g + 1) * SSM_STATE].astype(BF16) for g in range(SSM_GROUPS)]

    dt = _softplus(misc_ref[...] + dtb_ref[...])
    a_neg = -jnp.exp(alog_ref[...])
    d_a = dt * a_neg
    rr = lax.broadcasted_iota(jnp.int32, (L, L), 0)
    cc = lax.broadcasted_iota(jnp.int32, (L, L), 1)
    causal = cc <= rr
    tril = causal.astype(BF16)
    a_cum = sum(_dot(tril, piece) for piece in _split3(d_a))
    a_cum_t = a_cum.T
    lane = lax.broadcasted_iota(jnp.int32, (L, LANES), 1)
    b_t = [b.T.astype(F32) for b in b_in]
    cb = [_dot_nt(c_in[g], b_in[g].astype(BF16)) for g in range(SSM_GROUPS)]

    ys = []
    for pair in range(SSM_HEADS // 2):
        h0, h1 = 2 * pair, 2 * pair + 1
        dt_pair = jnp.where(lane < SSM_HEAD_DIM, dt[:, MISC_DT + h0:MISC_DT + h0 + 1],
                            dt[:, MISC_DT + h1:MISC_DT + h1 + 1])
        xs_pair = xs[:, pair * LANES:(pair + 1) * LANES]
        xdt = (xs_pair * dt_pair).astype(BF16)
        y_pair = []
        for h in (h0, h1):
            g = h // (SSM_HEADS // SSM_GROUPS)
            col = a_cum[:, MISC_DT + h:MISC_DT + h + 1]
            row = a_cum_t[MISC_DT + h:MISC_DT + h + 1, :]
            last = row[:, L - 1:L]
            decay = jnp.exp(jnp.where(causal, col - row, NEG_INF))
            y = _dot((cb[g] * decay).astype(BF16), xdt)
            y = y + _dot(c_in[g], state[h].astype(BF16)) * jnp.exp(col)
            new_state = state[h] * jnp.exp(last) + _dot((b_t[g] * jnp.exp(last - row)).astype(BF16), xdt)
            state[h] = new_state
            y_pair.append(y)
        ys.append(jnp.where(lane < SSM_HEAD_DIM, y_pair[0], y_pair[1]))
    y = jnp.concatenate(ys, axis=1) + dskip_ref[...] * xs
    zf = z_ref[...].astype(F32)
    y = y * (zf * jax.nn.sigmoid(zf))
    y = y * lax.rsqrt(jnp.mean(y * y, axis=-1, keepdims=True) + 1e-6) * g_ref[...]
    o_ref[...] = y.astype(o_ref.dtype)


def _ssd_mixer(proj, misc, conv_w, conv_b, dtb_pad, alog_pad, dskip_full, norm_g, B, S):
    T = B * S
    nc = S // SSD_L

    def row(width):
        return pl.BlockSpec((1, width), lambda b, c: (0, 0))

    return pl.pallas_call(
        _ssd_kernel,
        grid=(B, nc),
        in_specs=[pl.BlockSpec((SSD_L, D_SSM), lambda b, c: (b * nc + c, COL_Z // D_SSM)),
                  pl.BlockSpec((SSD_L, CONV_CH), lambda b, c: (b * nc + c, COL_XBC // CONV_CH)),
                  pl.BlockSpec((SSD_L, LANES), lambda b, c: (b * nc + c, 0)),
                  pl.BlockSpec((SUBLANES, CONV_CH), lambda b, c: (0, 0)),
                  row(CONV_CH), row(LANES), row(LANES), row(D_SSM), row(D_SSM)],
        out_specs=pl.BlockSpec((SSD_L, D_SSM), lambda b, c: (b * nc + c, 0)),
        out_shape=jax.ShapeDtypeStruct((T, D_SSM), F32),
        scratch_shapes=[pltpu.VMEM((SSD_L + SUBLANES, CONV_CH), F32),
                        pltpu.VMEM((SSM_HEADS, SSM_STATE, LANES), F32)],
        compiler_params=_cparams(("parallel", "arbitrary")),
        name="ssd_mixer",
    )(proj, proj, misc, conv_w, conv_b, dtb_pad, alog_pad, dskip_full, norm_g)


def _layer_norm_rows(v, g, b):
    mu = jnp.mean(v, axis=-1, keepdims=True)
    d = v - mu
    var = jnp.mean(d * d, axis=-1, keepdims=True)
    return d * lax.rsqrt(var + 1e-5) * g + b


def _rms_rows(v, g):
    return v * lax.rsqrt(jnp.mean(v * v, axis=-1, keepdims=True) + 1e-6) * g


def _route(logits_t):
    tm = logits_t.shape[1]
    m = jnp.max(logits_t, axis=0, keepdims=True)
    e = jnp.exp(logits_t - m)
    probs = e / jnp.sum(e, axis=0, keepdims=True)
    p3 = probs.reshape(N_EXPERT_GROUPS, EXPERTS_PER_GROUP, tm)
    io_e = lax.broadcasted_iota(jnp.int32, p3.shape, 1)

    def top2(p, io, n):
        m1 = jnp.max(p, axis=-2, keepdims=True)
        i1 = jnp.min(jnp.where(p == m1, io, n), axis=-2, keepdims=True)
        rest = jnp.where(io == i1, -1.0, p)
        m2 = jnp.max(rest, axis=-2, keepdims=True)
        i2 = jnp.min(jnp.where(rest == m2, io, n), axis=-2, keepdims=True)
        return m1, i1, m2, i2

    m1, _, m2, _ = top2(p3, io_e, EXPERTS_PER_GROUP)
    gscore = (m1 + m2)[:, 0, :]
    io_g = lax.broadcasted_iota(jnp.int32, gscore.shape, 0)
    gmax = jnp.max(gscore, axis=0, keepdims=True)
    gsel = jnp.min(jnp.where(gscore == gmax, io_g, N_EXPERT_GROUPS), axis=0, keepdims=True)
    in_group = jnp.sum(jnp.where(io_g[:, None, :] == gsel[None], p3, 0.0), axis=0)
    io_l = lax.broadcasted_iota(jnp.int32, in_group.shape, 0)
    w1, e1, w2, e2 = top2(in_group, io_l, EXPERTS_PER_GROUP)
    wsum = w1 + w2
    ids = jnp.concatenate([gsel * EXPERTS_PER_GROUP + e1, gsel * EXPERTS_PER_GROUP + e2], axis=0)
    wts = jnp.concatenate([w1 / wsum, w2 / wsum], axis=0)
    return ids, wts


def _outproj_kernel(onsa_ref, osb_ref, ossm_ref, x_ref, w_ref, gn_ref, gs_ref, lg_ref, lb_ref,
                    rw_ref, rb_ref, h_ref, hb_ref, ids_ref, wts_ref, *, alpha):
    on = _rms_rows(onsa_ref[...], gn_ref[...]).astype(BF16)
    os_ = _rms_rows(osb_ref[...], gs_ref[...]).astype(BF16)
    om = ossm_ref[...].astype(BF16)
    mix = (_dot(on, w_ref[0:D_NSA, :]) + _dot(os_, w_ref[D_NSA:D_NSA + D_SB, :])
           + _dot(om, w_ref[D_NSA + D_SB:, :]))
    h = _layer_norm_rows(alpha * x_ref[...] + mix, lg_ref[...], lb_ref[...])
    h_ref[...] = h
    hb_ref[...] = h.astype(BF16)
    h_hi, h_mid, _ = _split3(h)
    r_hi, r_mid, _ = _split3(rw_ref[...])
    logits_t = (_dot_nt(r_hi, h_hi) + _dot_nt(r_hi, h_mid) + _dot_nt(r_mid, h_hi)) + rb_ref[...]
    ids, wts = _route(logits_t)
    pad_i = jnp.zeros((SUBLANES - TOP_K, ids.shape[1]), jnp.int32)
    ids_ref[...] = jnp.concatenate([ids, pad_i], axis=0)
    wts_ref[...] = jnp.concatenate([wts, pad_i.astype(F32)], axis=0)


def _outproj_ln_router(onsa, osb, ossm, x2d, w_out, gn, gs, lg, lb, rw_t, rb, alpha, layer):
    T, D = x2d.shape
    tm = min(MM_TM, T)

    def rows(width):
        return pl.BlockSpec((tm, width), lambda i: (i, 0))

    def const(shape):
        return pl.BlockSpec(shape, lambda i: (0,) * len(shape))

    return pl.pallas_call(
        functools.partial(_outproj_kernel, alpha=alpha),
        grid=(T // tm,),
        in_specs=[rows(D_NSA), rows(D_SB), rows(D_SSM), rows(D),
                  pl.BlockSpec((None, D, D), lambda i: (layer, 0, 0)),
                  const((1, D_NSA)), const((1, D_SB)), const((1, D)), const((1, D)),
                  const((N_EXPERTS, D)), const((N_EXPERTS, 1))],
        out_specs=[rows(D), rows(D),
                   pl.BlockSpec((SUBLANES, tm), lambda i: (0, i)),
                   pl.BlockSpec((SUBLANES, tm), lambda i: (0, i))],
        out_shape=[jax.ShapeDtypeStruct((T, D), F32), jax.ShapeDtypeStruct((T, D), BF16),
                   jax.ShapeDtypeStruct((SUBLANES, T), jnp.int32),
                   jax.ShapeDtypeStruct((SUBLANES, T), F32)],
        compiler_params=_cparams(("parallel",)),
        name="outproj_ln_router",
    )(onsa, osb, ossm, x2d, w_out, gn, gs, lg, lb, rw_t, rb)


def _expert_kernel(be_ref, nu_ref, x_ref, wg_ref, wu_ref, wd_ref, o_ref):
    @pl.when(pl.program_id(0) < nu_ref[0])
    def _():
        x = x_ref[...]
        gate = _dot(x, wg_ref[...])
        up = _dot(x, wu_ref[...].astype(BF16))
        hid = (gate * jax.nn.sigmoid(gate) * up).astype(BF16)
        o_ref[...] = _dot(hid, wd_ref[...].astype(BF16)).astype(o_ref.dtype)

    @pl.when(pl.program_id(0) >= nu_ref[0])
    def _():
        o_ref[...] = jnp.zeros(o_ref.shape, o_ref.dtype)


def _expert_ffn(block_e, n_used, xs, wg, wu, wd, layer):
    n_rows, D = xs.shape
    n_blocks = n_rows // EXPERT_BLOCK
    d_ff = wg.shape[-1]
    grid_spec = pltpu.PrefetchScalarGridSpec(
        num_scalar_prefetch=2,
        grid=(n_blocks,),
        in_specs=[pl.BlockSpec((EXPERT_BLOCK, D), lambda i, be, nu: (i, 0)),
                  pl.BlockSpec((None, None, D, d_ff), lambda i, be, nu: (layer, be[i], 0, 0)),
                  pl.BlockSpec((None, None, D, d_ff), lambda i, be, nu: (layer, be[i], 0, 0)),
                  pl.BlockSpec((None, None, d_ff, D), lambda i, be, nu: (layer, be[i], 0, 0))],
        out_specs=pl.BlockSpec((EXPERT_BLOCK, D), lambda i, be, nu: (i, 0)),
    )
    return pl.pallas_call(
        _expert_kernel,
        grid_spec=grid_spec,
        out_shape=jax.ShapeDtypeStruct((n_rows, D), F32),
        compiler_params=_cparams(("arbitrary",)),
        name="expert_ffn",
    )(block_e, n_used, xs, wg, wu, wd)


def _ple_kernel(hb_ref, h_ref, p_ref, ffn_ref, wg_ref, wp_ref, lg_ref, lb_ref, o_ref, *, alpha):
    gate = jax.nn.sigmoid(_dot(hb_ref[...], wg_ref[...]))
    emb = _dot(p_ref[...].astype(BF16), wp_ref[...])
    v = alpha * h_ref[...] + ffn_ref[...] + gate * emb
    o_ref[...] = _layer_norm_rows(v, lg_ref[...], lb_ref[...])


def _ple_ln(hb, h, p3d, ffn, wg, wp, lg, lb, alpha, layer):
    T, D = h.shape
    tm = min(MM_TM, T)
    P = p3d.shape[-1]

    def rows(width):
        return pl.BlockSpec((tm, width), lambda i: (i, 0))

    def const(shape):
        return pl.BlockSpec(shape, lambda i: (0,) * len(shape))

    return pl.pallas_call(
        functools.partial(_ple_kernel, alpha=alpha),
        grid=(T // tm,),
        in_specs=[rows(D), rows(D), pl.BlockSpec((None, tm, P), lambda i: (layer, i, 0)), rows(D),
                  pl.BlockSpec((None, D, D), lambda i: (layer, 0, 0)),
                  pl.BlockSpec((None, P, D), lambda i: (layer, 0, 0)),
                  const((1, D)), const((1, D))],
        out_specs=rows(D),
        out_shape=jax.ShapeDtypeStruct((T, D), F32),
        compiler_params=_cparams(("parallel",)),
        name="ple_ln",
    )(hb, h, p3d, ffn, wg, wp, lg, lb)


def _dispatch_plan(ids, T):
    TK = T * TOP_K
    flat_e = ids.T.reshape(TK)
    order = jnp.argsort(flat_e)
    experts = jnp.arange(N_EXPERTS, dtype=jnp.int32)
    counts = jnp.sum((flat_e[:, None] == experts[None, :]).astype(jnp.int32), axis=0)
    padded = (counts + EXPERT_BLOCK - 1) // EXPERT_BLOCK * EXPERT_BLOCK
    start = jnp.cumsum(counts) - counts
    ends = jnp.cumsum(padded)
    pstart = ends - padded
    n_blocks = -(-TK // EXPERT_BLOCK) + N_EXPERTS
    n_rows = n_blocks * EXPERT_BLOCK
    blk0 = jnp.arange(n_blocks, dtype=jnp.int32) * EXPERT_BLOCK
    block_e = jnp.minimum(jnp.sum((ends[None, :] <= blk0[:, None]).astype(jnp.int32), axis=1), N_EXPERTS - 1)
    off = (blk0 - pstart[block_e])[:, None] + jnp.arange(EXPERT_BLOCK, dtype=jnp.int32)[None, :]
    valid = off < counts[block_e][:, None]
    src = jnp.clip(start[block_e][:, None] + off, 0, TK - 1)
    row_tok = jnp.where(valid, order[src] // TOP_K, 0).reshape(n_rows).astype(jnp.int32)
    se = flat_e[order]
    dest = pstart[se] + jnp.arange(TK, dtype=jnp.int32) - start[se]
    pos = dest[jnp.argsort(order)].reshape(T, TOP_K)
    n_used = (ends[-1] // EXPERT_BLOCK).astype(jnp.int32).reshape(1)
    return row_tok, pos, block_e.astype(jnp.int32), n_used


def _prep_w_in(w_in_l):
    d_kv = 6 * NSA_KV_HEADS * HEAD_DIM
    offs = np.cumsum([0, D_NSA, d_kv, NSA_HEADS * 3, 3 * D_SB, D_SSM, CONV_CH, SSM_HEADS])
    q, kv, gate, sbqkv, z, xbc, dt = [w_in_l[:, offs[i]:offs[i + 1]] for i in range(7)]
    scale = HEAD_DIM ** -0.5
    main = jnp.concatenate([q * scale, kv, sbqkv[:, :D_SB] * scale, sbqkv[:, D_SB:], z, xbc], axis=1)
    pad = jnp.zeros((w_in_l.shape[0], LANES - NSA_HEADS * 3 - SSM_HEADS), w_in_l.dtype)
    misc = jnp.concatenate([gate, dt, pad], axis=1)
    return main.astype(BF16), misc.astype(BF16)


def _pad_lanes(v, offset):
    out = jnp.zeros((1, LANES), F32)
    return lax.dynamic_update_slice(out, v.reshape(1, -1).astype(F32), (0, offset))


def kernel(x, p, w_in, w_out, cmp_pe_k, cmp_w1_k, cmp_w2_k, cmp_pe_v, cmp_w1_v, cmp_w2_v, nsa_norm_g, sb_norm_g, conv_w, conv_b, dt_bias, a_log, d_skip, ssm_norm_g, ln1_g, ln1_b, ln2_g, ln2_b, router_w, router_b, expert_w_gate, expert_w_up, expert_w_down, ple_gate_w, ple_proj_w):
    B, S, D = x.shape
    depth = w_in.shape[0]
    T = B * S
    alpha = (2 * depth) ** 0.25
    NC = S // NSA_CMP_STRIDE
    x2d = x.reshape(T, D)
    rw_t = router_w.T
    rb = router_b.reshape(N_EXPERTS, 1)
    w_out_b = w_out.astype(BF16)
    ew_gate_b = expert_w_gate.astype(BF16)
    ple_gate_b = ple_gate_w.astype(BF16)
    ple_proj_b = ple_proj_w.astype(BF16)
    p3d = p.reshape(depth, T, -1)

    for i in range(depth):
        w_main, w_misc = _prep_w_in(w_in[i])
        proj, misc = _inproj(x2d, w_main, w_misc)

        raw = proj[:, COL_KV:COL_KV + 2 * LANES].reshape(B, S, 2, NSA_KV_HEADS, HEAD_DIM)
        xr = raw.transpose(2, 0, 3, 1, 4).reshape(2, B, NSA_KV_HEADS, NC, NSA_CMP_STRIDE * HEAD_DIM)
        pe = jnp.stack([cmp_pe_k[i], cmp_pe_v[i]]).reshape(2, 1, NSA_CMP_LEN * HEAD_DIM)
        pe = jnp.broadcast_to(pe, (2, SUBLANES, NSA_CMP_LEN * HEAD_DIM))
        cmp = _nsa_compress(xr, pe, jnp.stack([cmp_w1_k[i], cmp_w1_v[i]]),
                            jnp.stack([cmp_w2_k[i], cmp_w2_v[i]]))
        cmp = cmp.transpose(0, 1, 3, 2, 4).reshape(2, B, NC, LANES)
        o_nsa = _nsa_attention(proj, misc, cmp[0], cmp[1], B, S)

        o_sb = _sb_attention(proj, B, S)

        o_ssm = _ssd_mixer(
            proj, misc,
            jnp.concatenate([conv_w[i].reshape(SSM_CONV, CONV_CH),
                             jnp.zeros((SUBLANES - SSM_CONV, CONV_CH), F32)], axis=0),
            conv_b[i].reshape(1, CONV_CH), _pad_lanes(dt_bias[i], MISC_DT), _pad_lanes(a_log[i], MISC_DT),
            jnp.repeat(d_skip[i], SSM_HEAD_DIM).reshape(1, D_SSM), ssm_norm_g[i].reshape(1, D_SSM), B, S)

        h, hb, ids, wts = _outproj_ln_router(
            o_nsa, o_sb, o_ssm, x2d, w_out_b, nsa_norm_g[i].reshape(1, D_NSA),
            sb_norm_g[i].reshape(1, D_SB), ln1_g[i].reshape(1, D), ln1_b[i].reshape(1, D), rw_t, rb, alpha, i)

        row_tok, pos, block_e, n_used = _dispatch_plan(ids[:TOP_K], T)
        xs = jnp.take(hb, row_tok, axis=0)
        ys = _expert_ffn(block_e, n_used, xs, ew_gate_b, expert_w_up, expert_w_down, i)
        w_tok = wts[:TOP_K].T
        ffn = (ys.at[pos[:, 0]].get(mode="promise_in_bounds") * w_tok[:, 0:1]
               + ys.at[pos[:, 1]].get(mode="promise_in_bounds") * w_tok[:, 1:2])

        x2d = _ple_ln(hb, h, p3d, ffn, ple_gate_b, ple_proj_b, ln2_g[i].reshape(1, D), ln2_b[i].reshape(1, D),
                      alpha, i)

    return x2d.reshape(B, S, D)
```

```python
import functools
import math

import jax
import jax.numpy as jnp
import numpy as np
from jax import lax
from jax.experimental import pallas as pl
from jax.experimental.pallas import tpu as pltpu

F32 = jnp.float32
BF16 = jnp.bfloat16

HEAD_DIM = 64
NSA_HEADS = 12
NSA_KV_HEADS = 2
NSA_HPG = NSA_HEADS // NSA_KV_HEADS
NSA_CMP_LEN = 32
NSA_CMP_STRIDE = 16
NSA_CMP_HIDDEN = 256
NSA_SEL_LEN = 64
NSA_N_SEL = 16
NSA_WINDOW = 512
FORCE_BONUS = 1e4
MASKED_SCORE = -1e9
NEG_INF = -1e30
SB_HEADS = 8
SSM_HEADS = 12
SSM_HEAD_DIM = 64
SSM_GROUPS = 2
SSM_STATE = 128
SSM_CONV = 4
D_NSA = NSA_HEADS * HEAD_DIM
D_SB = SB_HEADS * HEAD_DIM
D_SSM = SSM_HEADS * SSM_HEAD_DIM
CONV_CH = D_SSM + 2 * SSM_GROUPS * SSM_STATE
N_EXPERTS = 32
N_EXPERT_GROUPS = 4
EXPERTS_PER_GROUP = N_EXPERTS // N_EXPERT_GROUPS
TOP_K = 2

LANES = 128
SUBLANES = 8
VMEM_LIMIT_BYTES = 56 * 1024 * 1024

COL_Q = 0
COL_KV = COL_Q + D_NSA
COL_SBQ = COL_KV + 6 * LANES
COL_SBK = COL_SBQ + D_SB
COL_SBV = COL_SBK + D_SB
COL_Z = COL_SBV + D_SB
COL_XBC = COL_Z + D_SSM
N_MAIN = COL_XBC + CONV_CH
MISC_GATE = 0
MISC_DT = NSA_HEADS * 3

QB = 128
SEL_CHUNK = 512
SB_TQ = 256
SSD_L = 128
MM_TM = 256
EXPERT_BLOCK = 256
SB_SKIP_BELOW = -150.0


def _cparams(sem):
    return pltpu.CompilerParams(dimension_semantics=sem, vmem_limit_bytes=VMEM_LIMIT_BYTES)


def _split3(x):
    hi = x.astype(BF16)
    r1 = x - hi.astype(F32)
    mid = r1.astype(BF16)
    lo = (r1 - mid.astype(F32)).astype(BF16)
    return hi, mid, lo


def _dot(a, b):
    return jnp.dot(a, b, preferred_element_type=F32)


def _dot_nt(a, b):
    return lax.dot_general(a, b, (((1,), (1,)), ((), ())), preferred_element_type=F32)


def _inproj_kernel(x_ref, w_ref, wm_ref, o_ref, m_ref, xb_ref):
    @pl.when(pl.program_id(1) == 0)
    def _():
        xb_ref[...] = x_ref[...].astype(BF16)
        m_ref[...] = _dot(xb_ref[...], wm_ref[...])

    o_ref[...] = _dot(xb_ref[...], w_ref[...]).astype(o_ref.dtype)


def _inproj(x2d, w_main, w_misc, tm=1024, tn=640):
    T, D = x2d.shape
    tm = min(tm, T)
    return pl.pallas_call(
        _inproj_kernel,
        grid=(T // tm, N_MAIN // tn),
        in_specs=[pl.BlockSpec((tm, D), lambda i, j: (i, 0)),
                  pl.BlockSpec((D, tn), lambda i, j: (0, j)),
                  pl.BlockSpec((D, LANES), lambda i, j: (0, 0))],
        out_specs=[pl.BlockSpec((tm, tn), lambda i, j: (i, j)),
                   pl.BlockSpec((tm, LANES), lambda i, j: (i, 0))],
        out_shape=[jax.ShapeDtypeStruct((T, N_MAIN), BF16),
                   jax.ShapeDtypeStruct((T, LANES), F32)],
        scratch_shapes=[pltpu.VMEM((tm, D), BF16)],
        compiler_params=_cparams(("parallel", "arbitrary")),
        name="inproj",
    )(x2d, w_main, w_misc)


def _gelu_tanh(x):
    return 0.5 * x * (1.0 + jnp.tanh(math.sqrt(2.0 / math.pi) * (x + 0.044715 * (x * x * x))))


def _cmp_kernel(x_ref, pe_ref, w1_ref, w2_ref, o_ref):
    nc = x_ref.shape[0]
    half = x_ref.shape[1]
    w1 = w1_ref[...].astype(BF16)
    pq = _dot(x_ref[...], w1[:half]), _dot(x_ref[...], w1[half:])
    pe_term = _dot(pe_ref[...].astype(BF16), w1)[0:1]
    h = pq[0] + pltpu.roll(pq[1], nc - 1, axis=0) + pe_term
    o_ref[...] = _dot(_gelu_tanh(h).astype(BF16), w2_ref[...].astype(BF16)).astype(o_ref.dtype)


def _nsa_compress(xr, pe, w1, w2):
    _, B, G, NC, W = xr.shape
    return pl.pallas_call(
        _cmp_kernel,
        grid=(2, B, G),
        in_specs=[pl.BlockSpec((None, None, None, NC, W), lambda s, b, g: (s, b, g, 0, 0)),
                  pl.BlockSpec((None, SUBLANES, 2 * W), lambda s, b, g: (s, 0, 0)),
                  pl.BlockSpec((None, 2 * W, NSA_CMP_HIDDEN), lambda s, b, g: (s, 0, 0)),
                  pl.BlockSpec((None, NSA_CMP_HIDDEN, HEAD_DIM), lambda s, b, g: (s, 0, 0))],
        out_specs=pl.BlockSpec((None, None, None, NC, HEAD_DIM), lambda s, b, g: (s, b, g, 0, 0)),
        out_shape=jax.ShapeDtypeStruct((2, B, G, NC, HEAD_DIM), BF16),
        compiler_params=_cparams(("parallel", "parallel", "parallel")),
        name="nsa_compress",
    )(xr, pe, w1, w2)


def _alibi_slopes():
    return [float(2.0 ** (-8.0 * (i + 1) / NSA_HEADS)) for i in range(NSA_HEADS)]


def _row_reduce(x, op, lane_reduce):
    n = x.shape[-1]
    acc = x[..., 0:LANES]
    for i in range(1, n // LANES):
        acc = op(acc, x[..., i * LANES:(i + 1) * LANES])
    return lane_reduce(acc, axis=-1, keepdims=True)


def _row_max(x):
    return _row_reduce(x, jnp.maximum, jnp.max)


def _row_sum(x):
    return _row_reduce(x, jnp.add, jnp.sum)


def _softmax_masked(lg):
    m = _row_max(lg)
    e = jnp.exp(lg - m)
    return e * (1.0 / _row_sum(e))


def _nsa_kernel(q_ref, misc_ref, kc_ref, vc_ref, ks_ref, vs_ref, kw_ref, vw_ref, ovt_ref, o_ref,
                m_sc, acc_sc, *, seq):
    H = NSA_HPG
    G = NSA_KV_HEADS
    R = H * QB
    nc = kc_ref.shape[0]
    nb = seq // NSA_SEL_LEN
    n_sel = min(NSA_N_SEL, nb)
    bpc = SEL_CHUNK // NSA_SEL_LEN
    q0 = pl.program_id(1) * QB
    t_q = q0 + lax.broadcasted_iota(jnp.int32, (QB, 1), 0)
    t3 = t_q[None]
    lane = lax.broadcasted_iota(jnp.int32, (QB, LANES), 1)
    gates = jax.nn.sigmoid(misc_ref[...])
    slopes = _alibi_slopes()

    def head_bias(g, pos):
        slope3 = jnp.concatenate(
            [jnp.full((1, 1, 1), slopes[g * H + h], F32) for h in range(H)], axis=0)
        return slope3 * pos.astype(F32)

    qps = []
    for g in range(G):
        in_g = (lane >= g * HEAD_DIM) & (lane < (g + 1) * HEAD_DIM)
        parts = []
        for h in range(H):
            hh = g * H + h
            slab = q_ref[:, (hh // 2) * LANES:(hh // 2 + 1) * LANES].astype(F32)
            if hh % 2 != g:
                slab = pltpu.roll(slab, HEAD_DIM, axis=1)
            parts.append(jnp.where(in_g, slab, 0.0).astype(BF16))
        qps.append(jnp.concatenate(parts, axis=0))

    cmp_end = lax.broadcasted_iota(jnp.int32, (1, 1, nc), 2) * NSA_CMP_STRIDE + (NSA_CMP_LEN - 1)
    neg_c = jnp.where(t3 >= cmp_end, 0.0, NEG_INF)
    row_ok = (t_q >= NSA_CMP_LEN - 1).astype(F32)
    t_row = q0 + lax.broadcasted_iota(jnp.int32, (1, QB), 1)
    blk_t = lax.broadcasted_iota(jnp.int32, (nb, QB), 0)
    cur_t = t_row // NSA_SEL_LEN
    forced = (blk_t == 0) | (blk_t == cur_t) | (blk_t == cur_t - 1)
    free = (blk_t <= cur_t) & jnp.logical_not(forced)
    o_cs, sels_t = [], []
    for g in range(G):
        lc = _dot_nt(qps[g], kc_ref[...]).reshape(H, QB, nc)
        p_c = _softmax_masked(lc + head_bias(g, cmp_end) + neg_c) * row_ok[None]
        o_cs.append(_dot(p_c.reshape(R, nc).astype(BF16), vc_ref[...]))
        p_sum = jnp.sum(p_c, axis=0)
        imp_t = sum(_dot_nt(ovt_ref[...], piece) for piece in _split3(p_sum))
        score = jnp.where(free, imp_t, -3e38)
        sel_t = forced.astype(F32)
        for _ in range(n_sel - 3):
            mx = jnp.max(score, axis=0, keepdims=True)
            first = jnp.min(jnp.where(score == mx, blk_t, nb), axis=0, keepdims=True)
            pick = blk_t == first
            sel_t = jnp.where(pick, 1.0, sel_t)
            score = jnp.where(pick, -3e38, score)
        sels_t.append(sel_t)
    sel_bf = [s.T.astype(BF16) for s in sels_t]

    m_sc[...] = jnp.full(m_sc.shape, NEG_INF, F32)
    acc_sc[...] = jnp.zeros(acc_sc.shape, F32)
    n_chunks = (q0 + QB + SEL_CHUNK - 1) // SEL_CHUNK
    lane_row = lax.broadcasted_iota(jnp.int32, (1, LANES), 1)
    own_lanes = [(lane_row >= g * HEAD_DIM) & (lane_row < (g + 1) * HEAD_DIM) for g in range(G)]

    def with_ones(v, g):
        return jnp.where(own_lanes[g], v, jnp.ones((), v.dtype))

    def exp_tiles(lg, m_rep):
        n = lg.shape[-1] // LANES
        return jnp.concatenate([jnp.exp(lg[..., i * LANES:(i + 1) * LANES] - m_rep) for i in range(n)], axis=-1)

    def sel_step(i, carry):
        c = n_chunks - 1 - i
        k0 = pl.multiple_of(c * SEL_CHUNK, SEL_CHUNK)
        in_chunk = (blk_t // bpc) == c
        s_pos = k0 + lax.broadcasted_iota(jnp.int32, (1, 1, SEL_CHUNK), 2)
        causal = t3 >= s_pos
        ej = lax.broadcasted_iota(jnp.int32, (nb, SEL_CHUNK), 0)
        es = lax.broadcasted_iota(jnp.int32, (nb, SEL_CHUNK), 1)
        expand = (ej == c * bpc + es // NSA_SEL_LEN).astype(BF16)
        for g in range(G):
            hit = jnp.max(jnp.where(in_chunk, sels_t[g], 0.0))

            @pl.when(hit > 0.0)
            def _():
                key_sel = _dot(sel_bf[g], expand)
                neg = jnp.where((key_sel[None] > 0.5) & causal, 0.0, NEG_INF)
                ls = _dot_nt(qps[g], ks_ref[pl.ds(k0, SEL_CHUNK), :]).reshape(H, QB, SEL_CHUNK)
                lg = ls + head_bias(g, s_pos) + neg
                m_old = m_sc[g].reshape(H, QB, LANES)
                m_new = jnp.maximum(m_old, _row_max(lg))
                alpha = jnp.exp(m_old - m_new)
                pr = exp_tiles(lg, m_new)
                pv = _dot(pr.reshape(R, SEL_CHUNK).astype(BF16), with_ones(vs_ref[pl.ds(k0, SEL_CHUNK), :], g))
                acc_sc[g] = alpha.reshape(R, LANES) * acc_sc[g] + pv
                m_sc[g] = m_new.reshape(R, LANES)

        return carry

    lax.fori_loop(0, n_chunks, sel_step, 0)

    wk = NSA_WINDOW + QB
    w0 = pl.multiple_of(jnp.maximum(q0 - NSA_WINDOW, 0), QB)
    w_pos = w0 + lax.broadcasted_iota(jnp.int32, (1, 1, wk), 2)
    dist_w = t3 - w_pos
    neg_w = jnp.where((dist_w >= 0) & (dist_w < NSA_WINDOW), 0.0, NEG_INF)

    for g in range(G):
        acc = acc_sc[g]
        o_s = acc / pltpu.roll(acc, HEAD_DIM, axis=1)
        lw = _dot_nt(qps[g], kw_ref[pl.ds(w0, wk), :]).reshape(H, QB, wk) + head_bias(g, w_pos) + neg_w
        e_w = jnp.exp(lw - _row_max(lw))
        ow = _dot(e_w.reshape(R, wk).astype(BF16), with_ones(vw_ref[pl.ds(w0, wk), :], g))
        o_w = ow / pltpu.roll(ow, HEAD_DIM, axis=1)
        o_c = o_cs[g]

        outs = []
        for h in range(H):
            hh = g * H + h
            rows = slice(h * QB, (h + 1) * QB)
            gc = gates[:, hh * 3 + 0:hh * 3 + 1]
            gs = gates[:, hh * 3 + 1:hh * 3 + 2]
            gw = gates[:, hh * 3 + 2:hh * 3 + 3]
            outs.append(gc * o_c[rows] + gs * o_s[rows] + gw * o_w[rows])
        for pair in range(H // 2):
            a, b = outs[2 * pair], outs[2 * pair + 1]
            if g == 0:
                slab = jnp.where(lane < HEAD_DIM, a, pltpu.roll(b, HEAD_DIM, axis=1))
            else:
                slab = jnp.where(lane < HEAD_DIM, pltpu.roll(a, HEAD_DIM, axis=1), b)
            col = (g * H // 2 + pair) * LANES
            o_ref[:, col:col + LANES] = slab.astype(o_ref.dtype)


def _overlap_matrix_t(nc, nb):
    k = np.arange(nc)[:, None]
    j = np.arange(nb)[None, :]
    ratio = NSA_SEL_LEN // NSA_CMP_STRIDE
    ov = (k < ratio * j + ratio) & (k + NSA_CMP_LEN // NSA_CMP_STRIDE > ratio * j) & (k < nc - 1)
    return jnp.asarray(ov.T.astype(np.float32), dtype=BF16)


def _nsa_attention(proj, misc, kcmp, vcmp, B, S):
    T = B * S
    nqb = S // QB
    NC = kcmp.shape[1]
    nb = S // NSA_SEL_LEN
    assert min(NSA_N_SEL, nb) > 3 and S >= NSA_WINDOW + QB
    R = NSA_HPG * QB
    G = NSA_KV_HEADS
    kv0 = COL_KV // LANES

    def slab(j):
        return pl.BlockSpec((S, LANES), lambda b, i, j=j: (b, kv0 + j))

    return pl.pallas_call(
        functools.partial(_nsa_kernel, seq=S),
        grid=(B, nqb),
        in_specs=[pl.BlockSpec((QB, D_NSA), lambda b, i: (b * nqb + i, 0)),
                  pl.BlockSpec((QB, LANES), lambda b, i: (b * nqb + i, 0)),
                  pl.BlockSpec((None, NC, LANES), lambda b, i: (b, 0, 0)),
                  pl.BlockSpec((None, NC, LANES), lambda b, i: (b, 0, 0)),
                  slab(2), slab(3), slab(4), slab(5),
                  pl.BlockSpec((nb, NC), lambda b, i: (0, 0))],
        out_specs=pl.BlockSpec((QB, D_NSA), lambda b, i: (b * nqb + i, 0)),
        out_shape=jax.ShapeDtypeStruct((T, D_NSA), F32),
        scratch_shapes=[pltpu.VMEM((G, R, LANES), F32), pltpu.VMEM((G, R, LANES), F32)],
        compiler_params=_cparams(("parallel", "arbitrary")),
        name="nsa_attention",
    )(proj, misc, kcmp, vcmp, proj, proj, proj, proj, _overlap_matrix_t(NC, nb))


def _softplus(z):
    return jnp.maximum(z, 0.0) + jnp.log(1.0 + jnp.exp(-jnp.abs(z)))


def _sb_kernel(q_ref, k_ref, v_ref, o_ref, c_sc, acc_sc):
    tq = q_ref.shape[0]
    qi = pl.program_id(2)
    lane = lax.broadcasted_iota(jnp.int32, (tq, LANES), 1)
    jr = lax.broadcasted_iota(jnp.int32, (tq, tq), 0)
    sc = lax.broadcasted_iota(jnp.int32, (tq, tq), 1)
    upper = (jr > sc).astype(BF16)
    diag_mask = sc < jr
    qf = q_ref[...].astype(F32)
    res = []
    for h in range(2):
        in_h = (lane >= h * HEAD_DIM) & (lane < (h + 1) * HEAD_DIM)
        qp = jnp.where(in_h, qf, 0.0).astype(BF16)
        c_sc[...] = jnp.zeros(c_sc.shape, F32)
        acc_sc[...] = jnp.zeros(acc_sc.shape, F32)

        def block(kb, mask):
            k0 = pl.multiple_of(kb * tq, tq)
            z = _dot_nt(qp, k_ref[pl.ds(k0, tq), :])
            sp = _softplus(z)
            log1m = -sp if mask is None else jnp.where(mask, -sp, 0.0)
            hi, mid, lo = _split3(log1m)
            tail = _dot(hi, upper) + _dot(mid, upper) + _dot(lo, upper)
            arg = (z - sp) + tail + c_sc[...]
            a = jnp.exp(arg)
            if mask is not None:
                a = jnp.where(mask, a, 0.0)
            acc_sc[...] += _dot(a.astype(BF16), v_ref[pl.ds(k0, tq), :])
            c_sc[...] += jnp.sum(log1m, axis=-1, keepdims=True)

        block(qi, diag_mask)

        def cond(st):
            kb, cmax = st
            return (kb >= 0) & (cmax > SB_SKIP_BELOW)

        def body(st):
            kb, _ = st
            block(kb, None)
            return kb - 1, jnp.max(c_sc[...])

        lax.while_loop(cond, body, (qi - 1, jnp.max(c_sc[...])))
        res.append(acc_sc[...])
    o_ref[...] = jnp.where(lane < HEAD_DIM, res[0], res[1]).astype(o_ref.dtype)


def _sb_attention(proj, B, S):
    T = B * S
    tq = min(SB_TQ, S)
    nq = S // tq
    nslab = D_SB // LANES
    return pl.pallas_call(
        _sb_kernel,
        grid=(B, nslab, nq),
        in_specs=[pl.BlockSpec((tq, LANES), lambda b, s, i: (b * nq + i, COL_SBQ // LANES + s)),
                  pl.BlockSpec((S, LANES), lambda b, s, i: (b, COL_SBK // LANES + s)),
                  pl.BlockSpec((S, LANES), lambda b, s, i: (b, COL_SBV // LANES + s))],
        out_specs=pl.BlockSpec((tq, LANES), lambda b, s, i: (b * nq + i, s)),
        out_shape=jax.ShapeDtypeStruct((T, D_SB), F32),
        scratch_shapes=[pltpu.VMEM((tq, 1), F32), pltpu.VMEM((tq, LANES), F32)],
        compiler_params=_cparams(("parallel", "parallel", "arbitrary")),
        name="sb_attention",
    )(proj, proj, proj)


def _ssd_kernel(z_ref, xbc_ref, misc_ref, cw_ref, cb_ref, dtb_ref, alog_ref, dskip_ref, g_ref,
                o_ref, xbuf, state):
    L = SSD_L
    c = pl.program_id(1)

    @pl.when(c == 0)
    def _():
        xbuf[0:SUBLANES, :] = jnp.zeros((SUBLANES, CONV_CH), F32)
        state[...] = jnp.zeros(state.shape, F32)

    xbuf[SUBLANES:SUBLANES + L, :] = xbc_ref[...].astype(F32)
    conv = cb_ref[...]
    for w in range(SSM_CONV):
        off = SUBLANES - (SSM_CONV - 1) + w
        conv = conv + xbuf[off:off + L, :] * cw_ref[w:w + 1, :]
    xbuf[0:SUBLANES, :] = xbuf[L:L + SUBLANES, :]
    xbc = conv * jax.nn.sigmoid(conv)
    xs = xbc[:, :D_SSM]
    b_in = [xbc[:, D_SSM + g * SSM_STATE:D_SSM + (g + 1) * SSM_STATE] for g in range(SSM_GROUPS)]
    c0 = D_SSM + SSM_GROUPS * SSM_STATE
    c_in = [xbc[:, c0 + g * SSM_STATE:c0 + (g + 1) * SSM_STATE].astype(BF16) for g in range(SSM_GROUPS)]

    dt = _softplus(misc_ref[...] + dtb_ref[...])
    a_neg = -jnp.exp(alog_ref[...])
    d_a = dt * a_neg
    rr = lax.broadcasted_iota(jnp.int32, (L, L), 0)
    cc = lax.broadcasted_iota(jnp.int32, (L, L), 1)
    causal = cc <= rr
    tril = causal.astype(BF16)
    a_cum = sum(_dot(tril, piece) for piece in _split3(d_a))
    a_cum_t = a_cum.T
    lane = lax.broadcasted_iota(jnp.int32, (L, LANES), 1)
    b_t = [b.T.astype(F32) for b in b_in]
    cb = [_dot_nt(c_in[g], b_in[g].astype(BF16)) for g in range(SSM_GROUPS)]

    ys = []
    for pair in range(SSM_HEADS // 2):
        h0, h1 = 2 * pair, 2 * pair + 1
        dt_pair = jnp.where(lane < SSM_HEAD_DIM, dt[:, MISC_DT + h0:MISC_DT + h0 + 1],
                            dt[:, MISC_DT + h1:MISC_DT + h1 + 1])
        xs_pair = xs[:, pair * LANES:(pair + 1) * LANES]
        xdt = (xs_pair * dt_pair).astype(BF16)
        y_pair = []
        for h in (h0, h1):
            g = h // (SSM_HEADS // SSM_GROUPS)
            col = a_cum[:, MISC_DT + h:MISC_DT + h + 1]
            row = a_cum_t[MISC_DT + h:MISC_DT + h + 1, :]
            last = row[:, L - 1:L]
            decay = jnp.exp(jnp.where(causal, col - row, NEG_INF))
            y = _dot((cb[g] * decay).astype(BF16), xdt)
            y = y + _dot(c_in[g], state[h].astype(BF16)) * jnp.exp(col)
            new_state = state[h] * jnp.exp(last) + _dot((b_t[g] * jnp.exp(last - row)).astype(BF16), xdt)
            state[h] = new_state
            y_pair.append(y)
        ys.append(jnp.where(lane < SSM_HEAD_DIM, y_pair[0], y_pair[1]))
    y = jnp.concatenate(ys, axis=1) + dskip_ref[...] * xs
    zf = z_ref[...].astype(F32)
    y = y * (zf * jax.nn.sigmoid(zf))
    y = y * lax.rsqrt(jnp.mean(y * y, axis=-1, keepdims=True) + 1e-6) * g_ref[...]
    o_ref[...] = y.astype(o_ref.dtype)


def _ssd_mixer(proj, misc, conv_w, conv_b, dtb_pad, alog_pad, dskip_full, norm_g, B, S):
    T = B * S
    nc = S // SSD_L

    def row(width):
        return pl.BlockSpec((1, width), lambda b, c: (0, 0))

    return pl.pallas_call(
        _ssd_kernel,
        grid=(B, nc),
        in_specs=[pl.BlockSpec((SSD_L, D_SSM), lambda b, c: (b * nc + c, COL_Z // D_SSM)),
                  pl.BlockSpec((SSD_L, CONV_CH), lambda b, c: (b * nc + c, COL_XBC // CONV_CH)),
                  pl.BlockSpec((SSD_L, LANES), lambda b, c: (b * nc + c, 0)),
                  pl.BlockSpec((SUBLANES, CONV_CH), lambda b, c: (0, 0)),
                  row(CONV_CH), row(LANES), row(LANES), row(D_SSM), row(D_SSM)],
        out_specs=pl.BlockSpec((SSD_L, D_SSM), lambda b, c: (b * nc + c, 0)),
        out_shape=jax.ShapeDtypeStruct((T, D_SSM), F32),
        scratch_shapes=[pltpu.VMEM((SSD_L + SUBLANES, CONV_CH), F32),
                        pltpu.VMEM((SSM_HEADS, SSM_STATE, LANES), F32)],
        compiler_params=_cparams(("parallel", "arbitrary")),
        name="ssd_mixer",
    )(proj, proj, misc, conv_w, conv_b, dtb_pad, alog_pad, dskip_full, norm_g)


def _layer_norm_rows(v, g, b):
    mu = jnp.mean(v, axis=-1, keepdims=True)
    d = v - mu
    var = jnp.mean(d * d, axis=-1, keepdims=True)
    return d * lax.rsqrt(var + 1e-5) * g + b


def _rms_rows(v, g):
    return v * lax.rsqrt(jnp.mean(v * v, axis=-1, keepdims=True) + 1e-6) * g


def _route(logits_t):
    tm = logits_t.shape[1]
    m = jnp.max(logits_t, axis=0, keepdims=True)
    e = jnp.exp(logits_t - m)
    probs = e / jnp.sum(e, axis=0, keepdims=True)
    p3 = probs.reshape(N_EXPERT_GROUPS, EXPERTS_PER_GROUP, tm)
    io_e = lax.broadcasted_iota(jnp.int32, p3.shape, 1)

    def top2(p, io, n):
        m1 = jnp.max(p, axis=-2, keepdims=True)
        i1 = jnp.min(jnp.where(p == m1, io, n), axis=-2, keepdims=True)
        rest = jnp.where(io == i1, -1.0, p)
        m2 = jnp.max(rest, axis=-2, keepdims=True)
        i2 = jnp.min(jnp.where(rest == m2, io, n), axis=-2, keepdims=True)
        return m1, i1, m2, i2

    m1, _, m2, _ = top2(p3, io_e, EXPERTS_PER_GROUP)
    gscore = (m1 + m2)[:, 0, :]
    io_g = lax.broadcasted_iota(jnp.int32, gscore.shape, 0)
    gmax = jnp.max(gscore, axis=0, keepdims=True)
    gsel = jnp.min(jnp.where(gscore == gmax, io_g, N_EXPERT_GROUPS), axis=0, keepdims=True)
    in_group = jnp.sum(jnp.where(io_g[:, None, :] == gsel[None], p3, 0.0), axis=0)
    io_l = lax.broadcasted_iota(jnp.int32, in_group.shape, 0)
    w1, e1, w2, e2 = top2(in_group, io_l, EXPERTS_PER_GROUP)
    wsum = w1 + w2
    ids = jnp.concatenate([gsel * EXPERTS_PER_GROUP + e1, gsel * EXPERTS_PER_GROUP + e2], axis=0)
    wts = jnp.concatenate([w1 / wsum, w2 / wsum], axis=0)
    return ids, wts


def _outproj_kernel(onsa_ref, osb_ref, ossm_ref, x_ref, w_ref, gn_ref, gs_ref, lg_ref, lb_ref,
                    rw_ref, rb_ref, h_ref, hb_ref, ids_ref, wts_ref, *, alpha):
    on = _rms_rows(onsa_ref[...], gn_ref[...]).astype(BF16)
    os_ = _rms_rows(osb_ref[...], gs_ref[...]).astype(BF16)
    om = ossm_ref[...].astype(BF16)
    mix = (_dot(on, w_ref[0:D_NSA, :]) + _dot(os_, w_ref[D_NSA:D_NSA + D_SB, :])
           + _dot(om, w_ref[D_NSA + D_SB:, :]))
    h = _layer_norm_rows(alpha * x_ref[...] + mix, lg_ref[...], lb_ref[...])
    h_ref[...] = h
    hb_ref[...] = h.astype(BF16)
    h_hi, h_mid, _ = _split3(h)
    r_hi, r_mid, _ = _split3(rw_ref[...])
    logits_t = (_dot_nt(r_hi, h_hi) + _dot_nt(r_hi, h_mid) + _dot_nt(r_mid, h_hi)) + rb_ref[...]
    ids, wts = _route(logits_t)
    pad_i = jnp.zeros((SUBLANES - TOP_K, ids.shape[1]), jnp.int32)
    ids_ref[...] = jnp.concatenate([ids, pad_i], axis=0)
    wts_ref[...] = jnp.concatenate([wts, pad_i.astype(F32)], axis=0)


def _outproj_ln_router(onsa, osb, ossm, x2d, w_out, gn, gs, lg, lb, rw_t, rb, alpha, layer):
    T, D = x2d.shape
    tm = min(MM_TM, T)

    def rows(width):
        return pl.BlockSpec((tm, width), lambda i: (i, 0))

    def const(shape):
        return pl.BlockSpec(shape, lambda i: (0,) * len(shape))

    return pl.pallas_call(
        functools.partial(_outproj_kernel, alpha=alpha),
        grid=(T // tm,),
        in_specs=[rows(D_NSA), rows(D_SB), rows(D_SSM), rows(D),
                  pl.BlockSpec((None, D, D), lambda i: (layer, 0, 0)),
                  const((1, D_NSA)), const((1, D_SB)), const((1, D)), const((1, D)),
                  const((N_EXPERTS, D)), const((N_EXPERTS, 1))],
        out_specs=[rows(D), rows(D),
                   pl.BlockSpec((SUBLANES, tm), lambda i: (0, i)),
                   pl.BlockSpec((SUBLANES, tm), lambda i: (0, i))],
        out_shape=[jax.ShapeDtypeStruct((T, D), F32), jax.ShapeDtypeStruct((T, D), BF16),
                   jax.ShapeDtypeStruct((SUBLANES, T), jnp.int32),
                   jax.ShapeDtypeStruct((SUBLANES, T), F32)],
        compiler_params=_cparams(("parallel",)),
        name="outproj_ln_router",
    )(onsa, osb, ossm, x2d, w_out, gn, gs, lg, lb, rw_t, rb)


def _expert_kernel(be_ref, nu_ref, x_ref, wg_ref, wu_ref, wd_ref, o_ref):
    @pl.when(pl.program_id(0) < nu_ref[0])
    def _():
        x = x_ref[...]
        gate = _dot(x, wg_ref[...].astype(BF16))
        up = _dot(x, wu_ref[...].astype(BF16))
        hid = (gate * jax.nn.sigmoid(gate) * up).astype(BF16)
        o_ref[...] = _dot(hid, wd_ref[...].astype(BF16)).astype(o_ref.dtype)

    @pl.when(pl.program_id(0) >= nu_ref[0])
    def _():
        o_ref[...] = jnp.zeros(o_ref.shape, o_ref.dtype)


def _expert_ffn(block_e, n_used, xs, wg, wu, wd, layer):
    n_rows, D = xs.shape
    n_blocks = n_rows // EXPERT_BLOCK
    d_ff = wg.shape[-1]
    grid_spec = pltpu.PrefetchScalarGridSpec(
        num_scalar_prefetch=2,
        grid=(n_blocks,),
        in_specs=[pl.BlockSpec((EXPERT_BLOCK, D), lambda i, be, nu: (i, 0)),
                  pl.BlockSpec((None, None, D, d_ff), lambda i, be, nu: (layer, be[i], 0, 0)),
                  pl.BlockSpec((None, None, D, d_ff), lambda i, be, nu: (layer, be[i], 0, 0)),
                  pl.BlockSpec((None, None, d_ff, D), lambda i, be, nu: (layer, be[i], 0, 0))],
        out_specs=pl.BlockSpec((EXPERT_BLOCK, D), lambda i, be, nu: (i, 0)),
    )
    return pl.pallas_call(
        _expert_kernel,
        grid_spec=grid_spec,
        out_shape=jax.ShapeDtypeStruct((n_rows, D), F32),
        compiler_params=_cparams(("arbitrary",)),
        name="expert_ffn",
    )(block_e, n_used, xs, wg, wu, wd)


def _ple_kernel(hb_ref, h_ref, p_ref, ffn_ref, wg_ref, wp_ref, lg_ref, lb_ref, o_ref, *, alpha):
    gate = jax.nn.sigmoid(_dot(hb_ref[...], wg_ref[...]))
    emb = _dot(p_ref[...].astype(BF16), wp_ref[...])
    v = alpha * h_ref[...] + ffn_ref[...] + gate * emb
    o_ref[...] = _layer_norm_rows(v, lg_ref[...], lb_ref[...])


def _ple_ln(hb, h, p3d, ffn, wg, wp, lg, lb, alpha, layer):
    T, D = h.shape
    tm = min(MM_TM, T)
    P = p3d.shape[-1]

    def rows(width):
        return pl.BlockSpec((tm, width), lambda i: (i, 0))

    def const(shape):
        return pl.BlockSpec(shape, lambda i: (0,) * len(shape))

    return pl.pallas_call(
        functools.partial(_ple_kernel, alpha=alpha),
        grid=(T // tm,),
        in_specs=[rows(D), rows(D), pl.BlockSpec((None, tm, P), lambda i: (layer, i, 0)), rows(D),
                  pl.BlockSpec((None, D, D), lambda i: (layer, 0, 0)),
                  pl.BlockSpec((None, P, D), lambda i: (layer, 0, 0)),
                  const((1, D)), const((1, D))],
        out_specs=rows(D),
        out_shape=jax.ShapeDtypeStruct((T, D), F32),
        compiler_params=_cparams(("parallel",)),
        name="ple_ln",
    )(hb, h, p3d, ffn, wg, wp, lg, lb)


def _dispatch_plan(ids, T):
    TK = T * TOP_K
    flat_e = ids.T.reshape(TK)
    order = jnp.argsort(flat_e)
    experts = jnp.arange(N_EXPERTS, dtype=jnp.int32)
    counts = jnp.sum((flat_e[:, None] == experts[None, :]).astype(jnp.int32), axis=0)
    padded = (counts + EXPERT_BLOCK - 1) // EXPERT_BLOCK * EXPERT_BLOCK
    start = jnp.cumsum(counts) - counts
    ends = jnp.cumsum(padded)
    pstart = ends - padded
    n_blocks = -(-TK // EXPERT_BLOCK) + N_EXPERTS
    n_rows = n_blocks * EXPERT_BLOCK
    blk0 = jnp.arange(n_blocks, dtype=jnp.int32) * EXPERT_BLOCK
    block_e = jnp.minimum(jnp.sum((ends[None, :] <= blk0[:, None]).astype(jnp.int32), axis=1), N_EXPERTS - 1)
    off = (blk0 - pstart[block_e])[:, None] + jnp.arange(EXPERT_BLOCK, dtype=jnp.int32)[None, :]
    valid = off < counts[block_e][:, None]
    src = jnp.clip(start[block_e][:, None] + off, 0, TK - 1)
    row_tok = jnp.where(valid, order[src] // TOP_K, 0).reshape(n_rows).astype(jnp.int32)
    se = flat_e[order]
    dest = pstart[se] + jnp.arange(TK, dtype=jnp.int32) - start[se]
    pos = dest[jnp.argsort(order)].reshape(T, TOP_K)
    n_used = (ends[-1] // EXPERT_BLOCK).astype(jnp.int32).reshape(1)
    return row_tok, pos, block_e.astype(jnp.int32), n_used


def _prep_w_in(w_in_l):
    d_kv = 6 * NSA_KV_HEADS * HEAD_DIM
    offs = np.cumsum([0, D_NSA, d_kv, NSA_HEADS * 3, 3 * D_SB, D_SSM, CONV_CH, SSM_HEADS])
    q, kv, gate, sbqkv, z, xbc, dt = [w_in_l[:, offs[i]:offs[i + 1]] for i in range(7)]
    scale = HEAD_DIM ** -0.5
    main = jnp.concatenate([q * scale, kv, sbqkv[:, :D_SB] * scale, sbqkv[:, D_SB:], z, xbc], axis=1)
    pad = jnp.zeros((w_in_l.shape[0], LANES - NSA_HEADS * 3 - SSM_HEADS), w_in_l.dtype)
    misc = jnp.concatenate([gate, dt, pad], axis=1)
    return main.astype(BF16), misc.astype(BF16)


def _pad_lanes(v, offset):
    out = jnp.zeros((1, LANES), F32)
    return lax.dynamic_update_slice(out, v.reshape(1, -1).astype(F32), (0, offset))


def kernel(x, p, w_in, w_out, cmp_pe_k, cmp_w1_k, cmp_w2_k, cmp_pe_v, cmp_w1_v, cmp_w2_v, nsa_norm_g, sb_norm_g, conv_w, conv_b, dt_bias, a_log, d_skip, ssm_norm_g, ln1_g, ln1_b, ln2_g, ln2_b, router_w, router_b, expert_w_gate, expert_w_up, expert_w_down, ple_gate_w, ple_proj_w):
    B, S, D = x.shape
    depth = w_in.shape[0]
    T = B * S
    alpha = (2 * depth) ** 0.25
    NC = S // NSA_CMP_STRIDE
    x2d = x.reshape(T, D)
    rw_t = router_w.T
    rb = router_b.reshape(N_EXPERTS, 1)
    w_out_b = w_out.astype(BF16)
    ple_gate_b = ple_gate_w.astype(BF16)
    ple_proj_b = ple_proj_w.astype(BF16)
    p3d = p.reshape(depth, T, -1)

    for i in range(depth):
        w_main, w_misc = _prep_w_in(w_in[i])
        proj, misc = _inproj(x2d, w_main, w_misc)

        raw = proj[:, COL_KV:COL_KV + 2 * LANES].reshape(B, S, 2, NSA_KV_HEADS, HEAD_DIM)
        xr = raw.transpose(2, 0, 3, 1, 4).reshape(2, B, NSA_KV_HEADS, NC, NSA_CMP_STRIDE * HEAD_DIM)
        pe = jnp.stack([cmp_pe_k[i], cmp_pe_v[i]]).reshape(2, 1, NSA_CMP_LEN * HEAD_DIM)
        pe = jnp.broadcast_to(pe, (2, SUBLANES, NSA_CMP_LEN * HEAD_DIM))
        cmp = _nsa_compress(xr, pe, jnp.stack([cmp_w1_k[i], cmp_w1_v[i]]),
                            jnp.stack([cmp_w2_k[i], cmp_w2_v[i]]))
        cmp = cmp.transpose(0, 1, 3, 2, 4).reshape(2, B, NC, LANES)
        o_nsa = _nsa_attention(proj, misc, cmp[0], cmp[1], B, S)

        o_sb = _sb_attention(proj, B, S)

        o_ssm = _ssd_mixer(
            proj, misc,
            jnp.concatenate([conv_w[i].reshape(SSM_CONV, CONV_CH),
                             jnp.zeros((SUBLANES - SSM_CONV, CONV_CH), F32)], axis=0),
            conv_b[i].reshape(1, CONV_CH), _pad_lanes(dt_bias[i], MISC_DT), _pad_lanes(a_log[i], MISC_DT),
            jnp.repeat(d_skip[i], SSM_HEAD_DIM).reshape(1, D_SSM), ssm_norm_g[i].reshape(1, D_SSM), B, S)

        h, hb, ids, wts = _outproj_ln_router(
            o_nsa, o_sb, o_ssm, x2d, w_out_b, nsa_norm_g[i].reshape(1, D_NSA),
            sb_norm_g[i].reshape(1, D_SB), ln1_g[i].reshape(1, D), ln1_b[i].reshape(1, D), rw_t, rb, alpha, i)

        row_tok, pos, block_e, n_used = _dispatch_plan(ids[:TOP_K], T)
        xs = jnp.take(hb, row_tok, axis=0)
        ys = _expert_ffn(block_e, n_used, xs, expert_w_gate, expert_w_up, expert_w_down, i)
        w_tok = wts[:TOP_K].T
        ffn = (ys.at[pos[:, 0]].get(mode="promise_in_bounds") * w_tok[:, 0:1]
               + ys.at[pos[:, 1]].get(mode="promise_in_bounds") * w_tok[:, 1:2])

        x2d = _ple_ln(hb, h, p3d, ffn, ple_gate_b, ple_proj_b, ln2_g[i].reshape(1, D), ln2_b[i].reshape(1, D),
                      alpha, i)

    return x2d.reshape(B, S, D)
```
